```python
import math
import jax, jax.numpy as jnp
from jax import lax
import numpy as np

D_MODEL = 1024
BATCH = 8
SEQ = 2048
DEPTH = 4
DEC_BATCH = 128
DEC_SEQ = 8
PAST_LEN = 16384
PAGE_SIZE = 128

BRANCH_W = D_MODEL // 2
S5_W = BRANCH_W
S5_GROUP = 16
S5_GROUPS = S5_W // S5_GROUP
S5_STATE = 64
S5_DT_MIN = 1e-3
S5_DT_MAX = 1e-1
RWKV_W = BRANCH_W
RWKV_HEAD = 64
RWKV_HEADS = RWKV_W // RWKV_HEAD
DECAY_RANK = 64
ICLR_RANK = 64
GN_EPS = 64e-5
POOL_W = BRANCH_W
POOL_WINDOWS = (2, 4, 8, 16)
POOL_GROUPS = len(POOL_WINDOWS)
POOL_GW = POOL_W // POOL_GROUPS
POOL_BUF = max(POOL_WINDOWS) - 1
N_BRANCH = 3
RWKV_SHIFT_W = 3 * RWKV_W + DECAY_RANK + ICLR_RANK
IN_SIZES = (S5_W, S5_W, RWKV_SHIFT_W, RWKV_W, POOL_W, POOL_W, N_BRANCH * D_MODEL)
IN_W = sum(IN_SIZES)
IN_SPLITS = tuple(int(o) for o in np.cumsum(IN_SIZES)[:-1])
RWKV_SPLITS = tuple(int(o) for o in np.cumsum((RWKV_W, RWKV_W, RWKV_W, DECAY_RANK)))
NORM_EPS = 1e-6

kernel_name = "hybrid_s5_rwkv7_pool_gated_decoder_step"


def rmsnorm(x, g):
    xf = x.astype(jnp.float32)
    y = xf * lax.rsqrt(jnp.mean(xf * xf, axis=-1, keepdims=True) + NORM_EPS) * g.astype(jnp.float32)
    return y.astype(x.dtype)


def s5_mix(u, h0_re, h0_im, lam_re, lam_im, log_dt, b_re, b_im, c_re, c_im, d_skip, w_glu):
    f32 = jnp.float32
    bt, t, _ = u.shape
    uf = u.astype(f32).reshape(bt, t, S5_GROUPS, S5_GROUP)
    lr = lam_re.astype(f32)
    li = lam_im.astype(f32)
    dt = jnp.exp(log_dt.astype(f32))[:, None]
    mag = jnp.exp(lr * dt)
    ab_r = mag * jnp.cos(li * dt)
    ab_i = mag * jnp.sin(li * dt)
    den = lr * lr + li * li
    nr = ab_r - 1.0
    co_r = (nr * lr + ab_i * li) / den
    co_i = (ab_i * lr - nr * li) / den
    bu_r = jnp.einsum('btgc,gnc->btgn', uf, b_re.astype(f32))
    bu_i = jnp.einsum('btgc,gnc->btgn', uf, b_im.astype(f32))
    bb_r = co_r * bu_r - co_i * bu_i
    bb_i = co_r * bu_i + co_i * bu_r
    a_r = jnp.broadcast_to(ab_r, bb_r.shape)
    a_i = jnp.broadcast_to(ab_i, bb_i.shape)

    def combine(e1, e2):
        a1r, a1i, b1r, b1i = e1
        a2r, a2i, b2r, b2i = e2
        return (a2r * a1r - a2i * a1i, a2r * a1i + a2i * a1r,
                a2r * b1r - a2i * b1i + b2r, a2r * b1i + a2i * b1r + b2i)

    acr, aci, hr, hi = lax.associative_scan(combine, (a_r, a_i, bb_r, bb_i), axis=1)
    h0r = h0_re.astype(f32)[:, None]
    h0i = h0_im.astype(f32)[:, None]
    hr = hr + acr * h0r - aci * h0i
    hi = hi + acr * h0i + aci * h0r
    y = jnp.einsum('btgn,gcn->btgc', hr, c_re.astype(f32)) - jnp.einsum('btgn,gcn->btgc', hi, c_im.astype(f32))
    y = y + d_skip.astype(f32).reshape(S5_GROUPS, S5_GROUP) * uf
    y = jax.nn.gelu(y.reshape(bt, t, S5_W))
    y = y * jax.nn.sigmoid(y @ w_glu.astype(f32))
    return y.astype(u.dtype), hr[:, -1], hi[:, -1]


def rwkv_mix(p, shift_prev, s0, mu, w0, w_w2, a0, w_a2, k_k, k_a, r_k, gn_w, gn_b):
    f32 = jnp.float32
    bt, t, _ = p.shape
    pf = p.astype(f32)
    prev = jnp.concatenate([shift_prev.astype(f32)[:, None], pf[:, :-1]], axis=1)
    pm = pf + (prev - pf) * mu.astype(f32)
    r, k, v, lw, la = jnp.split(pm, RWKV_SPLITS, axis=-1)
    w_log = -jax.nn.softplus(-(w0.astype(f32) + jnp.tanh(lw) @ w_w2.astype(f32))) - 0.5
    decay = jnp.exp(-jnp.exp(w_log))
    a = jax.nn.sigmoid(a0.astype(f32) + la @ w_a2.astype(f32))
    hd = lambda z: z.reshape(bt, t, RWKV_HEADS, RWKV_HEAD)
    r, k, v, decay, a = hd(r), hd(k), hd(v), hd(decay), hd(a)
    kk = k * k_k.astype(f32).reshape(RWKV_HEADS, RWKV_HEAD)
    kk = kk * lax.rsqrt(jnp.maximum(jnp.sum(kk * kk, axis=-1, keepdims=True), 1e-24))
    k = k * (1.0 + (a - 1.0) * k_a.astype(f32).reshape(RWKV_HEADS, RWKV_HEAD))

    def step(s, inp):
        r_t, w_t, k_t, v_t, kk_t, a_t = inp
        sa = jnp.einsum('bhvk,bhk->bhv', s, -kk_t)
        s = s * w_t[:, :, None, :] + sa[..., None] * (kk_t * a_t)[:, :, None, :] + v_t[..., None] * k_t[:, :, None, :]
        return s, jnp.einsum('bhvk,bhk->bhv', s, r_t)

    tm = lambda z: jnp.moveaxis(z, 1, 0)
    s_fin, o = lax.scan(step, s0.astype(f32), (tm(r), tm(decay), tm(k), tm(v), tm(kk), tm(a)))
    o = jnp.moveaxis(o, 0, 1)
    mean = jnp.mean(o, axis=-1, keepdims=True)
    var = jnp.mean(jnp.square(o - mean), axis=-1, keepdims=True)
    o = (o - mean) * lax.rsqrt(var + GN_EPS) * gn_w.astype(f32).reshape(RWKV_HEADS, RWKV_HEAD) \
        + gn_b.astype(f32).reshape(RWKV_HEADS, RWKV_HEAD)
    o = o + jnp.sum(r * k * r_k.astype(f32), axis=-1, keepdims=True) * v
    return o.reshape(bt, t, RWKV_W).astype(p.dtype), p[:, -1], s_fin


def pool_mix(u, buf, start, pool_w, pool_scale):
    f32 = jnp.float32
    bt, t, _ = u.shape
    uf = u.astype(f32)
    ext = jnp.concatenate([buf.astype(f32), uf], axis=1)
    cs = jnp.concatenate([jnp.zeros((bt, 1, POOL_W), f32), jnp.cumsum(ext, axis=1)], axis=1)
    pos = start + jnp.arange(t)
    outs = []
    for gi, win in enumerate(POOL_WINDOWS):
        sl = slice(gi * POOL_GW, (gi + 1) * POOL_GW)
        end = cs[:, POOL_BUF + 1:, sl]
        beg = cs[:, POOL_BUF + 1 - win:POOL_BUF + 1 - win + t, sl]
        cnt = jnp.minimum(pos + 1, win).astype(f32)[None, :, None]
        outs.append((end - beg) / cnt)
    pooled = jnp.concatenate(outs, axis=-1) - uf
    mixed = jnp.einsum('btgc,gcd->btgd', pooled.reshape(bt, t, POOL_GROUPS, POOL_GW), pool_w.astype(f32))
    mixed = mixed.reshape(bt, t, POOL_W) * pool_scale.astype(f32)
    return mixed.astype(u.dtype), ext[:, -POOL_BUF:].astype(u.dtype)


def trunk_layer(x, s5_re, s5_im, rw_s, rw_shift, pool_buf, start,
                g_pre, g_post, w_in, lam_re, lam_im, log_dt, b_re, b_im, c_re, c_im, d_skip, w_glu,
                mu, w0, w_w2, a0, w_a2, k_k, k_a, r_k, gn_w, gn_b, pool_w, pool_scale,
                w_up_s5, w_up_rwkv, w_up_pool, w_o):
    bt, t, _ = x.shape
    h = rmsnorm(x, g_pre)
    s5_x, s5_z, rw_p, rw_z, pl_x, pl_z, gate_cols = jnp.split(h @ w_in, IN_SPLITS, axis=-1)
    y_s5, n_re, n_im = s5_mix(s5_x, s5_re, s5_im, lam_re, lam_im, log_dt, b_re, b_im, c_re, c_im, d_skip, w_glu)
    y_rw, n_shift, n_s = rwkv_mix(rw_p, rw_shift, rw_s, mu, w0, w_w2, a0, w_a2, k_k, k_a, r_k, gn_w, gn_b)
    y_pl, n_buf = pool_mix(pl_x, pool_buf, start, pool_w, pool_scale)
    u_s5 = (y_s5 * jax.nn.silu(s5_z)) @ w_up_s5
    u_rw = (y_rw * jax.nn.silu(rw_z)) @ w_up_rwkv
    u_pl = (y_pl * jax.nn.silu(pl_z)) @ w_up_pool
    gates = jax.nn.sigmoid(gate_cols.astype(jnp.float32)).reshape(bt, t, N_BRANCH, D_MODEL)
    merged = gates[:, :, 0] * u_s5 + gates[:, :, 1] * u_rw + gates[:, :, 2] * u_pl
    out = merged.astype(x.dtype) @ w_o
    return x + rmsnorm(out, g_post), n_re, n_im, n_s, n_shift, n_buf


def setup_inputs(seed: int = 0) -> dict:
    key = jax.random.key(seed)
    ks = iter(jax.random.split(key, 64))
    f32 = jnp.float32
    L, D = DEPTH, D_MODEL
    G, N, C = S5_GROUPS, S5_STATE, S5_GROUP

    def nrm(shape, scale):
        return jax.random.normal(next(ks), shape, f32) * scale

    return {
        "x_prompt": nrm((BATCH, SEQ, D), 1.0),
        "x_sample": nrm((DEC_BATCH, DEC_SEQ, D), 1.0),
        "state_s5_re": nrm((L, DEC_BATCH, G, N), 0.1),
        "state_s5_im": nrm((L, DEC_BATCH, G, N), 0.1),
        "state_rwkv": nrm((L, DEC_BATCH, RWKV_HEADS, RWKV_HEAD, RWKV_HEAD), 0.1),
        "state_shift": nrm((L, DEC_BATCH, RWKV_SHIFT_W), 1.0),
        "state_pool": nrm((L, DEC_BATCH, POOL_BUF, POOL_W), 1.0),
        "norm_pre": 1.0 + nrm((L, D), 0.05),
        "norm_post": 1.0 + nrm((L, D), 0.05),
        "w_in": nrm((L, D, IN_W), D ** -0.5),
        "s5_lam_re": -0.5 + nrm((L, G, N), 0.01),
        "s5_lam_im": math.pi * jnp.arange(N, dtype=f32) + nrm((L, G, N), 0.01),
        "s5_log_dt": jax.random.uniform(next(ks), (L, G), f32, math.log(S5_DT_MIN), math.log(S5_DT_MAX)),
        "s5_b_re": nrm((L, G, N, C), (2 * C) ** -0.5),
        "s5_b_im": nrm((L, G, N, C), (2 * C) ** -0.5),
        "s5_c_re": nrm((L, G, C, N), (2 * N) ** -0.5),
        "s5_c_im": nrm((L, G, C, N), (2 * N) ** -0.5),
        "s5_d": nrm((L, S5_W), 1.0),
        "s5_w_glu": nrm((L, S5_W, S5_W), S5_W ** -0.5),
        "rwkv_mu": jax.random.uniform(next(ks), (L, RWKV_SHIFT_W), f32),
        "rwkv_w0": jax.random.uniform(next(ks), (L, RWKV_W), f32, -5.0, 0.0),
        "rwkv_w_w2": nrm((L, DECAY_RANK, RWKV_W), 0.1),
        "rwkv_a0": nrm((L, RWKV_W), 0.1),
        "rwkv_w_a2": nrm((L, ICLR_RANK, RWKV_W), 0.1),
        "rwkv_k_k": 0.85 + nrm((L, RWKV_W), 0.02),
        "rwkv_k_a": 1.0 + nrm((L, RWKV_W), 0.02),
        "rwkv_r_k": nrm((L, RWKV_HEADS, RWKV_HEAD), 0.1),
        "rwkv_gn_w": 1.0 + nrm((L, RWKV_W), 0.05),
        "rwkv_gn_b": nrm((L, RWKV_W), 0.01),
        "pool_w": nrm((L, POOL_GROUPS, POOL_GW, POOL_GW), POOL_GW ** -0.5),
        "pool_scale": 1.0 + nrm((L, POOL_W), 0.1),
        "w_up_s5": nrm((L, S5_W, D), S5_W ** -0.5),
        "w_up_rwkv": nrm((L, RWKV_W, D), RWKV_W ** -0.5),
        "w_up_pool": nrm((L, POOL_W, D), POOL_W ** -0.5),
        "w_o": nrm((L, D, D), D ** -0.5),
    }


def reference(x_prompt, x_sample, state_s5_re, state_s5_im, state_rwkv, state_shift, state_pool,
              norm_pre, norm_post, w_in, s5_lam_re, s5_lam_im, s5_log_dt, s5_b_re, s5_b_im,
              s5_c_re, s5_c_im, s5_d, s5_w_glu, rwkv_mu, rwkv_w0, rwkv_w_w2, rwkv_a0, rwkv_w_a2,
              rwkv_k_k, rwkv_k_a, rwkv_r_k, rwkv_gn_w, rwkv_gn_b, pool_w, pool_scale,
              w_up_s5, w_up_rwkv, w_up_pool, w_o):
    f32 = jnp.float32
    xp, xs = x_prompt, x_sample
    bp = xp.shape[0]
    outs_p = [[] for _ in range(5)]
    outs_s = [[] for _ in range(5)]
    for l in range(DEPTH):
        wl = (norm_pre[l], norm_post[l], w_in[l], s5_lam_re[l], s5_lam_im[l], s5_log_dt[l],
              s5_b_re[l], s5_b_im[l], s5_c_re[l], s5_c_im[l], s5_d[l], s5_w_glu[l],
              rwkv_mu[l], rwkv_w0[l], rwkv_w_w2[l], rwkv_a0[l], rwkv_w_a2[l], rwkv_k_k[l], rwkv_k_a[l],
              rwkv_r_k[l], rwkv_gn_w[l], rwkv_gn_b[l], pool_w[l], pool_scale[l],
              w_up_s5[l], w_up_rwkv[l], w_up_pool[l], w_o[l])
        zeros_p = (jnp.zeros((bp, S5_GROUPS, S5_STATE), f32), jnp.zeros((bp, S5_GROUPS, S5_STATE), f32),
                   jnp.zeros((bp, RWKV_HEADS, RWKV_HEAD, RWKV_HEAD), f32),
                   jnp.zeros((bp, RWKV_SHIFT_W), xp.dtype), jnp.zeros((bp, POOL_BUF, POOL_W), xp.dtype))
        xp, *new_p = trunk_layer(xp, *zeros_p, 0, *wl)
        xs, *new_s = trunk_layer(xs, state_s5_re[l], state_s5_im[l], state_rwkv[l], state_shift[l],
                                 state_pool[l], PAST_LEN, *wl)
        for lst, val in zip(outs_p, new_p):
            lst.append(val)
        for lst, val in zip(outs_s, new_s):
            lst.append(val)
    p_s5_re, p_s5_im, p_rwkv, p_shift, p_pool = [jnp.stack(v) for v in outs_p]
    s_s5_re, s_s5_im, s_rwkv, s_shift, s_pool = [jnp.stack(v) for v in outs_s]
    return (xp, xs, p_s5_re, p_s5_im, p_rwkv, p_shift, p_pool, s_s5_re, s_s5_im, s_rwkv, s_shift, s_pool)
```

```python
import functools
import math

import jax
import jax.numpy as jnp
import numpy as np
from jax import lax
from jax.experimental import pallas as pl
from jax.experimental.pallas import tpu as pltpu

F32 = jnp.float32
BF16 = jnp.bfloat16

D_MODEL = 1024
BRANCH_W = 512
S5_GROUPS = 32
S5_GROUP = 16
S5_STATE = 64
S5_NSTATE = S5_GROUPS * S5_STATE
RWKV_HEADS = 8
RWKV_HEAD = 64
LORA_RANK = 64
RWKV_SHIFT_W = 3 * BRANCH_W + 2 * LORA_RANK
POOL_WINDOWS = (2, 4, 8, 16)
POOL_GW = 128
POOL_BUF = 15
NORM_EPS = 1e-6
GN_EPS = 64e-5
PAST_LEN = 16384

MXU_TILE = 256
SLAB_HEADS = MXU_TILE // RWKV_HEAD
N_SLABS = RWKV_HEADS // SLAB_HEADS
CHUNK = 64
VMEM_LIMIT = 56 * 1024 * 1024


def _dot(a, b):
    return jnp.dot(a, b, preferred_element_type=F32)


def _dot_nt(a, b):
    return lax.dot_general(a, b, (((1,), (1,)), ((), ())), preferred_element_type=F32)


def _dot_tn(a, b):
    return lax.dot_general(a, b, (((0,), (0,)), ((), ())), preferred_element_type=F32)


def _split3(x):
    hi = x.astype(BF16)
    r1 = x - hi.astype(F32)
    mid = r1.astype(BF16)
    lo = (r1 - mid.astype(F32)).astype(BF16)
    return hi, mid, lo


def _rms_scale(x, g):
    ms = jnp.mean(x * x, axis=-1, keepdims=True)
    return x * lax.rsqrt(ms + NORM_EPS) * g


def _sigmoid(x):
    return jax.nn.sigmoid(x)


def _silu(x):
    return x * jax.nn.sigmoid(x)


def _gelu_tanh(x):
    c = math.sqrt(2.0 / math.pi)
    return 0.5 * x * (1.0 + jnp.tanh(c * (x + 0.044715 * (x * x * x))))


def _full_spec(shape):
    nd = len(shape)
    return pl.BlockSpec(shape, lambda *_: (0,) * nd)


def _params(sem):
    return pltpu.CompilerParams(dimension_semantics=sem, vmem_limit_bytes=VMEM_LIMIT)


def _s5_body(x_ref, g_ref, w_ref, wb_ref, ar_ref, ai_ref, cc_ref, d_ref, wglu_ref, h0r_ref, h0i_ref,
             y_ref, nr_ref, ni_ref, bbr, bbi, cr, ci, *, nt, nb):
    j = pl.program_id(1)
    rows = nt * nb
    half = S5_NSTATE // 2

    @pl.when(j == 0)
    def _():
        cr[...] = h0r_ref[...]
        ci[...] = h0i_ref[...]

    x = x_ref[...].reshape(rows, D_MODEL)
    h = _rms_scale(x, g_ref[...]).astype(BF16)
    sxz = _dot(h, w_ref[...])
    sx = sxz[:, :BRANCH_W]
    sz = sxz[:, BRANCH_W:]
    ub = sx.astype(BF16)
    for k in range(2):
        o = _dot(ub[:, MXU_TILE * k:MXU_TILE * (k + 1)], wb_ref[k])
        bbr[:, half * k:half * (k + 1)] = o[:, :half]
        bbi[:, half * k:half * (k + 1)] = o[:, half:]

    cw = max(128, min(1024, (8 * 1024) // nb))
    for c0 in range(0, S5_NSTATE, cw):
        cs = slice(c0, c0 + cw)
        a_r = jnp.broadcast_to(ar_ref[:, cs], (nb, cw))
        a_i = jnp.broadcast_to(ai_ref[:, cs], (nb, cw))

        def step(t, carry, cs=cs, a_r=a_r, a_i=a_i):
            hr, hi = carry
            rs = pl.ds(pl.multiple_of(t * nb, nb), nb)
            nhr = a_r * hr - a_i * hi + bbr[rs, cs]
            nhi = a_r * hi + a_i * hr + bbi[rs, cs]
            bbr[rs, cs] = nhr
            bbi[rs, cs] = nhi
            return nhr, nhi

        hr, hi = lax.fori_loop(0, nt, step, (cr[:, cs], ci[:, cs]), unroll=min(nt, 8))
        cr[:, cs] = hr
        ci[:, cs] = hi

    ys = []
    for k in range(2):
        ks = slice(half * k, half * (k + 1))
        lhs = jnp.concatenate([bbr[:, ks].astype(BF16), bbi[:, ks].astype(BF16)], axis=1)
        ys.append(_dot(lhs, cc_ref[k]))
    y = jnp.concatenate(ys, axis=1) + d_ref[...] * sx
    y = _gelu_tanh(y)
    y = y * _sigmoid(_dot(y.astype(BF16), wglu_ref[...]))
    y_ref[...] = (y * _silu(sz)).astype(y_ref.dtype).reshape(y_ref.shape)
    nr_ref[...] = cr[...]
    ni_ref[...] = ci[...]


def _s5_call(x, wl, h0r, h0i, *, seq, batch, nt, nb, y_dtype):
    rows = nt * nb
    grid = (batch // nb, seq // nt)
    if nb == batch:
        x_spec = pl.BlockSpec((rows, D_MODEL), lambda i, j: (j, 0))
        y_spec = pl.BlockSpec((rows, BRANCH_W), lambda i, j: (j, 0))
        y_shape = jax.ShapeDtypeStruct((seq * batch, BRANCH_W), y_dtype)
    else:
        x_spec = pl.BlockSpec((nt, nb, D_MODEL), lambda i, j: (j, i, 0))
        y_spec = pl.BlockSpec((nt, nb, BRANCH_W), lambda i, j: (j, i, 0))
        y_shape = jax.ShapeDtypeStruct((seq, batch, BRANCH_W), y_dtype)
    st_spec = pl.BlockSpec((nb, S5_NSTATE), lambda i, j: (i, 0))
    st_shape = jax.ShapeDtypeStruct((batch, S5_NSTATE), F32)
    weights = (wl["g_pre"], wl["w_s5"], wl["s5_wb"], wl["s5_ar"], wl["s5_ai"], wl["s5_cc"], wl["s5_d"],
               wl["s5_wglu"])
    return pl.pallas_call(
        functools.partial(_s5_body, nt=nt, nb=nb),
        grid=grid,
        in_specs=[x_spec] + [_full_spec(w.shape) for w in weights] + [st_spec, st_spec],
        out_specs=[y_spec, st_spec, st_spec],
        out_shape=[y_shape, st_shape, st_shape],
        scratch_shapes=[pltpu.VMEM((rows, S5_NSTATE), F32), pltpu.VMEM((rows, S5_NSTATE), F32),
                        pltpu.VMEM((nb, S5_NSTATE), F32), pltpu.VMEM((nb, S5_NSTATE), F32)],
        compiler_params=_params(("arbitrary", "arbitrary")),
    )(x, *weights, h0r, h0i)


def _pool_body(x_ref, g_ref, w_ref, pw_ref, ps_ref, buf_ref, y_ref, nbuf_ref, ext, *, nt, nb, start):
    j = pl.program_id(1)
    rows = nt * nb
    prev = POOL_BUF * nb

    @pl.when(j == 0)
    def _():
        ext[0:prev, :] = buf_ref[...].reshape(prev, BRANCH_W)

    x = x_ref[...].reshape(rows, D_MODEL)
    h = _rms_scale(x, g_ref[...]).astype(BF16)
    pxz = _dot(h, w_ref[...])
    px = pxz[:, :BRANCH_W]
    pz = pxz[:, BRANCH_W:]
    ext[prev:prev + rows, :] = px

    t_local = lax.broadcasted_iota(jnp.int32, (rows, POOL_GW), 0) // nb
    pos1 = start + j * nt + t_local + 1
    outs = []
    for gi, win in enumerate(POOL_WINDOWS):
        cs = slice(POOL_GW * gi, POOL_GW * (gi + 1))
        acc = ext[prev:prev + rows, cs]
        for q in range(1, win):
            acc = acc + ext[prev - q * nb:prev - q * nb + rows, cs]
        cnt = jnp.minimum(pos1, win).astype(F32)
        outs.append(acc / cnt)
    pooled = jnp.concatenate(outs, axis=1) - px
    mixed = _dot(pooled.astype(BF16), pw_ref[...]) * ps_ref[...]
    y_ref[...] = (mixed * _silu(pz)).astype(y_ref.dtype).reshape(y_ref.shape)
    last = ext[rows:rows + prev, :]
    nbuf_ref[...] = last.reshape(nbuf_ref.shape)
    ext[0:prev, :] = last


def _pool_call(x, wl, buf, *, seq, batch, nt, nb, start, y_dtype):
    rows = nt * nb
    grid = (batch // nb, seq // nt)
    if nb == batch:
        x_spec = pl.BlockSpec((rows, D_MODEL), lambda i, j: (j, 0))
        y_spec = pl.BlockSpec((rows, BRANCH_W), lambda i, j: (j, 0))
        y_shape = jax.ShapeDtypeStruct((seq * batch, BRANCH_W), y_dtype)
        b_spec = pl.BlockSpec((POOL_BUF * nb, BRANCH_W), lambda i, j: (0, 0))
        b_shape = jax.ShapeDtypeStruct((POOL_BUF * batch, BRANCH_W), F32)
    else:
        x_spec = pl.BlockSpec((nt, nb, D_MODEL), lambda i, j: (j, i, 0))
        y_spec = pl.BlockSpec((nt, nb, BRANCH_W), lambda i, j: (j, i, 0))
        y_shape = jax.ShapeDtypeStruct((seq, batch, BRANCH_W), y_dtype)
        b_spec = pl.BlockSpec((POOL_BUF, nb, BRANCH_W), lambda i, j: (0, i, 0))
        b_shape = jax.ShapeDtypeStruct((POOL_BUF, batch, BRANCH_W), F32)
    weights = (wl["g_pre"], wl["w_pool"], wl["pool_w"], wl["pool_scale"])
    return pl.pallas_call(
        functools.partial(_pool_body, nt=nt, nb=nb, start=start),
        grid=grid,
        in_specs=[x_spec] + [_full_spec(w.shape) for w in weights] + [b_spec],
        out_specs=[y_spec, b_spec],
        out_shape=[y_shape, b_shape],
        scratch_shapes=[pltpu.VMEM(((POOL_BUF + nt) * nb, BRANCH_W), F32)],
        compiler_params=_params(("arbitrary", "arbitrary")),
    )(x, *weights, buf)


def _proj_body(x_ref, g_ref, w_ref, p_ref, z_ref):
    h = _rms_scale(x_ref[...], g_ref[...]).astype(BF16)
    o = _dot(h, w_ref[...])
    p_ref[...] = o[:, :RWKV_SHIFT_W]
    z_ref[...] = o[:, RWKV_SHIFT_W:]


def _proj_call(x, wl, *, tm):
    m = x.shape[0]
    weights = (wl["g_pre"], wl["w_rw"])
    return pl.pallas_call(
        _proj_body,
        grid=(m // tm,),
        in_specs=[pl.BlockSpec((tm, D_MODEL), lambda i: (i, 0))] + [_full_spec(w.shape) for w in weights],
        out_specs=[pl.BlockSpec((tm, RWKV_SHIFT_W), lambda i: (i, 0)),
                   pl.BlockSpec((tm, BRANCH_W), lambda i: (i, 0))],
        out_shape=[jax.ShapeDtypeStruct((m, RWKV_SHIFT_W), F32), jax.ShapeDtypeStruct((m, BRANCH_W), F32)],
        compiler_params=_params(("parallel",)),
    )(x, *weights)


def _seg_sum(x, ones):
    hi = x.astype(BF16)
    lo = (x - hi.astype(F32)).astype(BF16)
    outs = []
    for g in range(N_SLABS):
        ls = slice(MXU_TILE * g, MXU_TILE * (g + 1))
        outs.append(_dot(hi[:, ls], ones) + _dot(lo[:, ls], ones))
    return jnp.concatenate(outs, axis=1)


def _block_rows(y, lane_masks):
    return jnp.concatenate([jnp.where(m, y, 0.0).astype(BF16) for m in lane_masks], axis=0)


def _rw_body(p_ref, z_ref, sh_ref, s0_ref, mu_ref, w0_ref, wl_ref, a0_ref, kkw_ref, kaw_ref, rkw_ref,
             gnw_ref, gnb_ref, ones_ref, tri_ref,
             y_ref, sout_ref,
             sbd, carry, qk_s, qr_s, kb_s, kk_s, v_s, pc_s, o_s,
             *, rows, seq_len, nseq, single_seq):
    j = pl.program_id(1)
    n_chunks = rows // CHUNK
    log_seq = seq_len.bit_length() - 1

    lane256 = lax.broadcasted_iota(jnp.int32, (CHUNK, MXU_TILE), 1)
    row64 = lax.broadcasted_iota(jnp.int32, (CHUNK, MXU_TILE), 0)
    src = lane256 & (CHUNK - 1)
    same_seq = (row64 >> log_seq) == (src >> log_seq)
    m_strict = same_seq & (src < row64)
    m_incl = same_seq & (src <= row64)
    lane_masks = [(lane256 >> 6) == hh for hh in range(SLAB_HEADS)]
    brow = lax.broadcasted_iota(jnp.int32, (MXU_TILE, MXU_TILE), 0)
    bcol = lax.broadcasted_iota(jnp.int32, (MXU_TILE, MXU_TILE), 1)
    block_mask = (brow >> 6) == (bcol >> 6)
    expand = ((lax.broadcasted_iota(jnp.int32, (RWKV_HEAD, MXU_TILE), 1) & (RWKV_HEAD - 1))
              == lax.broadcasted_iota(jnp.int32, (RWKV_HEAD, MXU_TILE), 0)).astype(BF16)

    def load_state(q, src_rows):
        for g in range(N_SLABS):
            s2 = src_rows[MXU_TILE * g:MXU_TILE * (g + 1), :]
            hi, mid, lo = _split3(s2)
            wide = _dot(hi, expand) + _dot(mid, expand) + _dot(lo, expand)
            sbd[q, g] = jnp.where(block_mask, wide, 0.0)

    def store_state(q):
        parts = []
        for g in range(N_SLABS):
            hi, mid, lo = _split3(sbd[q, g])
            parts.append(_dot_nt(hi, expand) + _dot_nt(mid, expand) + _dot_nt(lo, expand))
        return jnp.concatenate(parts, axis=0)

    if single_seq:
        @pl.when(j == 0)
        def _():
            load_state(0, s0_ref[0])
            carry[...] = sh_ref[0]

        p = p_ref[...]
        z = z_ref[...]
        first = jnp.broadcast_to(carry[...], (rows, RWKV_SHIFT_W))
        is_first = lax.broadcasted_iota(jnp.int32, (rows, RWKV_SHIFT_W), 0) == 0
    else:
        for q in range(nseq):
            load_state(q, s0_ref[q])
        p = jnp.concatenate([p_ref[:, RWKV_SHIFT_W * q:RWKV_SHIFT_W * (q + 1)] for q in range(nseq)], axis=0)
        z = jnp.concatenate([z_ref[:, BRANCH_W * q:BRANCH_W * (q + 1)] for q in range(nseq)], axis=0)
        first = jnp.concatenate(
            [jnp.broadcast_to(sh_ref[q:q + 1, :], (seq_len, RWKV_SHIFT_W)) for q in range(nseq)], axis=0)
        is_first = (lax.broadcasted_iota(jnp.int32, (rows, RWKV_SHIFT_W), 0) & (seq_len - 1)) == 0

    prev = jnp.where(is_first, first, pltpu.roll(p, 1, 0))
    if single_seq:
        carry[...] = p[rows - 1:rows, :]
    pm = p + (prev - p) * mu_ref[...]
    r = pm[:, 0:BRANCH_W]
    k = pm[:, BRANCH_W:2 * BRANCH_W]
    v = pm[:, 2 * BRANCH_W:3 * BRANCH_W]
    ll = pm[:, 3 * BRANCH_W:]
    lane128 = lax.broadcasted_iota(jnp.int32, (rows, 2 * LORA_RANK), 1)
    xl = jnp.where(lane128 < LORA_RANK, jnp.tanh(ll), ll).astype(BF16)
    dl = _dot(xl, wl_ref[...])
    w_log = -jax.nn.softplus(-(w0_ref[...] + dl[:, :BRANCH_W])) - 0.5
    ld = -jnp.exp(w_log)
    a = _sigmoid(a0_ref[...] + dl[:, BRANCH_W:])
    ones = ones_ref[...]
    kk = k * kkw_ref[...]
    kkn = kk * lax.rsqrt(jnp.maximum(_seg_sum(kk * kk, ones), 1e-24))
    k2 = k * (1.0 + (a - 1.0) * kaw_ref[...])
    bv = kkn * a
    hi, mid, lo = _split3(ld)
    tri = tri_ref[...]
    cum = _dot(tri, hi) + _dot(tri, mid) + _dot(tri, lo)
    e_incl = jnp.exp(cum)
    e_inv = jnp.exp(-cum)
    qk_s[...] = kkn * jnp.exp(cum - ld)
    qr_s[...] = r * e_incl
    kb_s[...] = bv * e_inv
    kk_s[...] = k2 * e_inv
    v_s[...] = v
    pc_s[...] = e_incl

    def chunk(c):
        r0 = c * CHUNK if isinstance(c, int) else pl.multiple_of(c * CHUNK, CHUNK)
        rs = pl.ds(r0, CHUNK)
        for g in range(N_SLABS):
            ls = slice(MXU_TILE * g, MXU_TILE * (g + 1))
            q_k = qk_s[rs, ls]
            q_r = qr_s[rs, ls]
            k_b = kb_s[rs, ls]
            k_k = kk_s[rs, ls]
            vv = v_s[rs, ls]
            q2 = jnp.concatenate([q_k.astype(BF16), q_r.astype(BF16)], axis=0)
            a_b = _dot_nt(q2, _block_rows(k_b, lane_masks))
            a_k = _dot_nt(q2, _block_rows(k_k, lane_masks))
            a_kb = jnp.where(m_strict, a_b[:CHUNK], 0.0)
            a_rb = jnp.where(m_incl, a_b[CHUNK:], 0.0)
            a_kk = jnp.where(m_strict, a_k[:CHUNK], 0.0)
            a_rk = jnp.where(m_incl, a_k[CHUNK:], 0.0)
            if single_seq:
                x0 = _dot_nt(q2, sbd[0, g].astype(BF16))
                x0k = x0[:CHUNK]
                x0r = x0[CHUNK:]
            else:
                xk, xr = [], []
                for q in range(nseq):
                    qs = slice(seq_len * q, seq_len * (q + 1))
                    lhs = jnp.concatenate([q_k[qs], q_r[qs]], axis=0).astype(BF16)
                    xq = _dot_nt(lhs, sbd[q, g].astype(BF16))
                    xk.append(xq[:seq_len])
                    xr.append(xq[seq_len:])
                x0k = jnp.concatenate(xk, axis=0)
                x0r = jnp.concatenate(xr, axis=0)
            bd_v = _block_rows(vv, lane_masks)
            rhs = -(x0k + _dot(a_kk.astype(BF16), bd_v))
            npow = -a_kb
            u = rhs
            for it in range(log_seq):
                if it + 1 < log_seq:
                    both = _dot(npow.astype(BF16), jnp.concatenate(
                        [_block_rows(u, lane_masks), _block_rows(npow, lane_masks)], axis=1))
                    u = u + both[:, :MXU_TILE]
                    npow = both[:, MXU_TILE:]
                else:
                    u = u + _dot(npow.astype(BF16), _block_rows(u, lane_masks))
            o = x0r + _dot(a_rb.astype(BF16), _block_rows(u, lane_masks)) + _dot(a_rk.astype(BF16), bd_v)
            o_s[rs, ls] = o
            if single_seq:
                uv = jnp.concatenate([u.astype(BF16), vv.astype(BF16)], axis=0)
                kb2 = jnp.concatenate([k_b.astype(BF16), k_k.astype(BF16)], axis=0)
                zz = _dot_tn(uv, kb2)
                pc = pc_s[pl.ds(r0 + CHUNK - 1, 1), ls]
                sbd[0, g] = (sbd[0, g] + jnp.where(block_mask, zz, 0.0)) * pc
            else:
                for q in range(nseq):
                    qs = slice(seq_len * q, seq_len * (q + 1))
                    uv = jnp.concatenate([u[qs], vv[qs]], axis=0).astype(BF16)
                    kb2 = jnp.concatenate([k_b[qs], k_k[qs]], axis=0).astype(BF16)
                    zz = _dot_tn(uv, kb2)
                    pc = pc_s[pl.ds(r0 + seq_len * (q + 1) - 1, 1), ls]
                    sbd[q, g] = (sbd[q, g] + jnp.where(block_mask, zz, 0.0)) * pc

    if n_chunks == 1:
        chunk(0)
    else:
        def loop_body(c, carry_):
            chunk(c)
            return carry_
        lax.fori_loop(0, n_chunks, loop_body, 0)

    o = o_s[...]
    mean = _seg_sum(o, ones) * (1.0 / RWKV_HEAD)
    dev = o - mean
    var = _seg_sum(dev * dev, ones) * (1.0 / RWKV_HEAD)
    o = dev * lax.rsqrt(var + GN_EPS) * gnw_ref[...] + gnb_ref[...]
    o = o + _seg_sum(r * k2 * rkw_ref[...], ones) * v
    yv = o * _silu(z)
    if single_seq:
        y_ref[...] = yv.astype(y_ref.dtype)
        sout_ref[0] = store_state(0)
    else:
        for q in range(nseq):
            y_ref[:, BRANCH_W * q:BRANCH_W * (q + 1)] = yv[seq_len * q:seq_len * (q + 1)].astype(y_ref.dtype)
            sout_ref[q] = store_state(q)


def _tri_matrix(rows, seq_len):
    t = np.arange(rows)
    m = ((t[:, None] // seq_len) == (t[None, :] // seq_len)) & (t[None, :] <= t[:, None])
    return jnp.asarray(m, dtype=BF16)


def _rw_call(p, z, shift0, s0, wl, *, seq, batch, single_seq, y_dtype):
    p2 = p.reshape(seq, batch * RWKV_SHIFT_W)
    z2 = z.reshape(seq, batch * BRANCH_W)
    if single_seq:
        rows = min(seq, 256)
        seq_len = CHUNK
        nseq = 1
        grid = (batch, seq // rows)
        p_spec = pl.BlockSpec((rows, RWKV_SHIFT_W), lambda b, j: (j, b))
        z_spec = pl.BlockSpec((rows, BRANCH_W), lambda b, j: (j, b))
        sh_in = shift0.reshape(batch, 1, RWKV_SHIFT_W)
        sh_spec = pl.BlockSpec((1, 1, RWKV_SHIFT_W), lambda b, j: (b, 0, 0))
        carry_shape = (1, RWKV_SHIFT_W)
    else:
        seq_len = seq
        nseq = CHUNK // seq
        rows = CHUNK
        grid = (batch // nseq, 1)
        p_spec = pl.BlockSpec((seq, nseq * RWKV_SHIFT_W), lambda b, j: (0, b))
        z_spec = pl.BlockSpec((seq, nseq * BRANCH_W), lambda b, j: (0, b))
        sh_in = shift0
        sh_spec = pl.BlockSpec((nseq, RWKV_SHIFT_W), lambda b, j: (b, 0))
        carry_shape = (8, 128)
    st_spec = pl.BlockSpec((nseq, BRANCH_W, RWKV_HEAD), lambda b, j: (b, 0, 0))
    tri = _tri_matrix(rows, seq_len)
    weights = (wl["rw_mu"], wl["rw_w0"], wl["rw_lora"], wl["rw_a0"], wl["rw_kk"], wl["rw_ka"], wl["rw_rk"],
               wl["rw_gnw"], wl["rw_gnb"], wl["rw_ones"], tri)
    work = pltpu.VMEM((rows, BRANCH_W), F32)
    y2, s_out = pl.pallas_call(
        functools.partial(_rw_body, rows=rows, seq_len=seq_len, nseq=nseq, single_seq=single_seq),
        grid=grid,
        in_specs=[p_spec, z_spec, sh_spec, st_spec] + [_full_spec(w.shape) for w in weights],
        out_specs=[z_spec, st_spec],
        out_shape=[jax.ShapeDtypeStruct((seq, batch * BRANCH_W), y_dtype),
                   jax.ShapeDtypeStruct((batch, BRANCH_W, RWKV_HEAD), F32)],
        scratch_shapes=[pltpu.VMEM((nseq, N_SLABS, MXU_TILE, MXU_TILE), F32), pltpu.VMEM(carry_shape, F32),
                        work, work, work, work, work, work, work],
        compiler_params=_params(("arbitrary", "arbitrary")),
    )(p2, z2, sh_in, s0, *weights)
    return y2.reshape(seq * batch, BRANCH_W), s_out


def _merge_body(x_ref, ys5_ref, yrw_ref, ypl_ref, gpre_ref, wg_ref, wu_ref, wo_ref, gpost_ref, o_ref):
    x = x_ref[...]
    h = _rms_scale(x, gpre_ref[...]).astype(BF16)
    merged = None
    for i, y_ref in enumerate((ys5_ref, yrw_ref, ypl_ref)):
        gate = _sigmoid(_dot(h, wg_ref[:, D_MODEL * i:D_MODEL * (i + 1)]))
        term = gate * _dot(y_ref[...].astype(BF16), wu_ref[i])
        merged = term if merged is None else merged + term
    out = _dot(merged.astype(BF16), wo_ref[...])
    o_ref[...] = x + _rms_scale(out, gpost_ref[...])


def _merge_call(x, ys5, yrw, ypl, wl, *, tm):
    m = x.shape[0]
    weights = (wl["g_pre"], wl["w_gate"], wl["w_up"], wl["w_o"], wl["g_post"])
    row_spec = lambda w: pl.BlockSpec((tm, w), lambda i: (i, 0))
    return pl.pallas_call(
        _merge_body,
        grid=(m // tm,),
        in_specs=[row_spec(D_MODEL), row_spec(BRANCH_W), row_spec(BRANCH_W), row_spec(BRANCH_W)]
        + [_full_spec(w.shape) for w in weights],
        out_specs=row_spec(D_MODEL),
        out_shape=jax.ShapeDtypeStruct((m, D_MODEL), F32),
        compiler_params=_params(("parallel",)),
    )(x, ys5, yrw, ypl, *weights)


def _block_diag(blocks):
    n, r, c = blocks.shape
    eye = jnp.eye(n, dtype=blocks.dtype)
    return jnp.einsum("grc,gh->grhc", blocks, eye).reshape(n * r, n * c)


def _prep_layer(l, norm_pre, norm_post, w_in, s5_lam_re, s5_lam_im, s5_log_dt, s5_b_re, s5_b_im,
                s5_c_re, s5_c_im, s5_d, s5_w_glu, rwkv_mu, rwkv_w0, rwkv_w_w2, rwkv_a0, rwkv_w_a2,
                rwkv_k_k, rwkv_k_a, rwkv_r_k, rwkv_gn_w, rwkv_gn_b, pool_w, pool_scale,
                w_up_s5, w_up_rwkv, w_up_pool, w_o):
    row = lambda a: a.astype(F32).reshape(1, -1)
    wl = {}
    wl["g_pre"] = row(norm_pre[l])
    wl["g_post"] = row(norm_post[l])
    win = w_in[l].astype(BF16)
    wl["w_s5"] = win[:, 0:1024]
    wl["w_rw"] = win[:, 1024:3200]
    wl["w_pool"] = win[:, 3200:4224]
    wl["w_gate"] = win[:, 4224:7296]

    lr = s5_lam_re[l].astype(F32)
    li = s5_lam_im[l].astype(F32)
    dt = jnp.exp(s5_log_dt[l].astype(F32))[:, None]
    mag = jnp.exp(lr * dt)
    ab_r = mag * jnp.cos(li * dt)
    ab_i = mag * jnp.sin(li * dt)
    den = lr * lr + li * li
    nr = ab_r - 1.0
    co_r = ((nr * lr + ab_i * li) / den)[:, :, None]
    co_i = ((ab_i * lr - nr * li) / den)[:, :, None]
    b_re = s5_b_re[l].astype(F32)
    b_im = s5_b_im[l].astype(F32)
    bf_re = co_r * b_re - co_i * b_im
    bf_im = co_r * b_im + co_i * b_re
    gh = S5_GROUPS // 2
    wb = []
    for k in range(2):
        gs = slice(gh * k, gh * (k + 1))
        wb.append(jnp.concatenate([_block_diag(jnp.swapaxes(bf_re[gs], 1, 2)),
                                   _block_diag(jnp.swapaxes(bf_im[gs], 1, 2))], axis=1))
    wl["s5_wb"] = jnp.stack(wb).astype(BF16)
    wl["s5_ar"] = ab_r.reshape(1, S5_NSTATE)
    wl["s5_ai"] = ab_i.reshape(1, S5_NSTATE)
    c_re = s5_c_re[l].astype(F32)
    c_im = s5_c_im[l].astype(F32)
    cc = []
    for k in range(2):
        gs = slice(gh * k, gh * (k + 1))
        cc.append(jnp.concatenate([_block_diag(jnp.swapaxes(c_re[gs], 1, 2)),
                                   _block_diag(jnp.swapaxes(-c_im[gs], 1, 2))], axis=0))
    wl["s5_cc"] = jnp.stack(cc).astype(BF16)
    wl["s5_d"] = row(s5_d[l])
    wl["s5_wglu"] = s5_w_glu[l].astype(BF16)

    wl["rw_mu"] = row(rwkv_mu[l])
    wl["rw_w0"] = row(rwkv_w0[l])
    zeros = jnp.zeros((LORA_RANK, BRANCH_W), F32)
    wl["rw_lora"] = jnp.concatenate(
        [jnp.concatenate([rwkv_w_w2[l].astype(F32), zeros], axis=1),
         jnp.concatenate([zeros, rwkv_w_a2[l].astype(F32)], axis=1)], axis=0).astype(BF16)
    wl["rw_a0"] = row(rwkv_a0[l])
    wl["rw_kk"] = row(rwkv_k_k[l])
    wl["rw_ka"] = row(rwkv_k_a[l])
    wl["rw_rk"] = row(rwkv_r_k[l])
    wl["rw_gnw"] = row(rwkv_gn_w[l])
    wl["rw_gnb"] = row(rwkv_gn_b[l])
    wl["rw_ones"] = _block_diag(jnp.ones((SLAB_HEADS, RWKV_HEAD, RWKV_HEAD), F32)).astype(BF16)

    wl["pool_w"] = _block_diag(pool_w[l].astype(F32)).astype(BF16)
    wl["pool_scale"] = row(pool_scale[l])
    wl["w_up"] = jnp.stack([w_up_s5[l], w_up_rwkv[l], w_up_pool[l]]).astype(BF16)
    wl["w_o"] = w_o[l].astype(BF16)
    return wl


def _layer(x, states, wl, *, seq, batch, start, prompt):
    s5r, s5i, rws, rwsh, pbuf = states
    if prompt:
        nt, nb, y_dtype = 64, batch, BF16
    else:
        nt, nb, y_dtype = seq, min(batch, 32), F32
    if nb == batch:
        x_tm = x
        pbuf_in = pbuf.reshape(POOL_BUF * batch, BRANCH_W)
    else:
        x_tm = x.reshape(seq, batch, D_MODEL)
        pbuf_in = pbuf
    ys5, n_re, n_im = _s5_call(x_tm, wl, s5r, s5i, seq=seq, batch=batch, nt=nt, nb=nb, y_dtype=y_dtype)
    ypl, n_buf = _pool_call(x_tm, wl, pbuf_in, seq=seq, batch=batch, nt=nt, nb=nb, start=start, y_dtype=y_dtype)
    p, z = _proj_call(x, wl, tm=512)
    yrw, n_s = _rw_call(p, z, rwsh, rws, wl, seq=seq, batch=batch, single_seq=prompt, y_dtype=y_dtype)
    x_new = _merge_call(x, ys5.reshape(seq * batch, BRANCH_W), yrw, ypl.reshape(seq * batch, BRANCH_W), wl, tm=512)
    n_shift = p[(seq - 1) * batch:, :]
    n_buf = n_buf.reshape(POOL_BUF, batch, BRANCH_W)
    return x_new, (n_re, n_im, n_s, n_shift, n_buf)


def kernel(x_prompt, x_sample, state_s5_re, state_s5_im, state_rwkv, state_shift, state_pool, norm_pre, norm_post, w_in, s5_lam_re, s5_lam_im, s5_log_dt, s5_b_re, s5_b_im, s5_c_re, s5_c_im, s5_d, s5_w_glu, rwkv_mu, rwkv_w0, rwkv_w_w2, rwkv_a0, rwkv_w_a2, rwkv_k_k, rwkv_k_a, rwkv_r_k, rwkv_gn_w, rwkv_gn_b, pool_w, pool_scale, w_up_s5, w_up_rwkv, w_up_pool, w_o):
    depth = w_in.shape[0]
    bp, tp, _ = x_prompt.shape
    bs, ts, _ = x_sample.shape
    xp = jnp.transpose(x_prompt, (1, 0, 2)).reshape(tp * bp, D_MODEL)
    xs = jnp.transpose(x_sample, (1, 0, 2)).reshape(ts * bs, D_MODEL)
    outs_p = [[] for _ in range(5)]
    outs_s = [[] for _ in range(5)]
    for l in range(depth):
        wl = _prep_layer(l, norm_pre, norm_post, w_in, s5_lam_re, s5_lam_im, s5_log_dt, s5_b_re, s5_b_im,
                         s5_c_re, s5_c_im, s5_d, s5_w_glu, rwkv_mu, rwkv_w0, rwkv_w_w2, rwkv_a0, rwkv_w_a2,
                         rwkv_k_k, rwkv_k_a, rwkv_r_k, rwkv_gn_w, rwkv_gn_b, pool_w, pool_scale,
                         w_up_s5, w_up_rwkv, w_up_pool, w_o)
        zeros_p = (jnp.zeros((bp, S5_NSTATE), F32), jnp.zeros((bp, S5_NSTATE), F32),
                   jnp.zeros((bp, BRANCH_W, RWKV_HEAD), F32), jnp.zeros((bp, RWKV_SHIFT_W), F32),
                   jnp.zeros((POOL_BUF, bp, BRANCH_W), F32))
        xp, new_p = _layer(xp, zeros_p, wl, seq=tp, batch=bp, start=0, prompt=True)
        st_s = (state_s5_re[l].reshape(bs, S5_NSTATE), state_s5_im[l].reshape(bs, S5_NSTATE),
                state_rwkv[l].reshape(bs, BRANCH_W, RWKV_HEAD), state_shift[l],
                jnp.transpose(state_pool[l], (1, 0, 2)))
        xs, new_s = _layer(xs, st_s, wl, seq=ts, batch=bs, start=PAST_LEN, prompt=False)
        for lst, val in zip(outs_p, new_p):
            lst.append(val)
        for lst, val in zip(outs_s, new_s):
            lst.append(val)

    def finish(outs, b):
        re, im, rw, sh, pb = [jnp.stack(v) for v in outs]
        return (re.reshape(depth, b, S5_GROUPS, S5_STATE), im.reshape(depth, b, S5_GROUPS, S5_STATE),
                rw.reshape(depth, b, RWKV_HEADS, RWKV_HEAD, RWKV_HEAD), sh,
                jnp.transpose(pb, (0, 2, 1, 3)))

    y_p = jnp.transpose(xp.reshape(tp, bp, D_MODEL), (1, 0, 2))
    y_s = jnp.transpose(xs.reshape(ts, bs, D_MODEL), (1, 0, 2))
    return (y_p, y_s) + finish(outs_p, bp) + finish(outs_s, bs)
```

```python
import functools
import math

import jax
import jax.numpy as jnp
import numpy as np
from jax import lax
from jax.experimental import pallas as pl
from jax.experimental.pallas import tpu as pltpu

F32 = jnp.float32
BF16 = jnp.bfloat16

D_MODEL = 1024
BRANCH_W = 512
S5_GROUPS = 32
S5_GROUP = 16
S5_STATE = 64
S5_NSTATE = S5_GROUPS * S5_STATE
RWKV_HEADS = 8
RWKV_HEAD = 64
LORA_RANK = 64
RWKV_SHIFT_W = 3 * BRANCH_W + 2 * LORA_RANK
POOL_WINDOWS = (2, 4, 8, 16)
POOL_GW = 128
POOL_BUF = 15
NORM_EPS = 1e-6
GN_EPS = 64e-5
PAST_LEN = 16384

MXU_TILE = 256
LANES = 128
CHUNK = 64
RW_GROUP = 512
RW_UNROLL = 2
VMEM_LIMIT = 56 * 1024 * 1024


def _dot(a, b):
    return jnp.dot(a, b, preferred_element_type=F32)


def _dot_nt(a, b):
    return lax.dot_general(a, b, (((1,), (1,)), ((), ())), preferred_element_type=F32)


def _dot_tn(a, b):
    return lax.dot_general(a, b, (((0,), (0,)), ((), ())), preferred_element_type=F32)


def _split3(x):
    hi = x.astype(BF16)
    r1 = x - hi.astype(F32)
    mid = r1.astype(BF16)
    lo = (r1 - mid.astype(F32)).astype(BF16)
    return hi, mid, lo


def _rms_scale(x, g):
    ms = jnp.mean(x * x, axis=-1, keepdims=True)
    return x * lax.rsqrt(ms + NORM_EPS) * g


def _sigmoid(x):
    return jax.nn.sigmoid(x)


def _silu(x):
    return x * jax.nn.sigmoid(x)


def _gelu_tanh(x):
    c = math.sqrt(2.0 / math.pi)
    return 0.5 * x * (1.0 + jnp.tanh(c * (x + 0.044715 * (x * x * x))))


def _full_spec(shape):
    nd = len(shape)
    return pl.BlockSpec(shape, lambda *_: (0,) * nd)


def _params(sem):
    return pltpu.CompilerParams(dimension_semantics=sem, vmem_limit_bytes=VMEM_LIMIT)


def _s5_body(x_ref, g_ref, w_ref, wb_ref, ar_ref, ai_ref, cc_ref, d_ref, wglu_ref, h0r_ref, h0i_ref,
             y_ref, nr_ref, ni_ref, bbr, bbi, cr, ci, *, nt, nb):
    j = pl.program_id(1)
    rows = nt * nb
    half = S5_NSTATE // 2

    @pl.when(j == 0)
    def _():
        cr[...] = h0r_ref[...]
        ci[...] = h0i_ref[...]

    x = x_ref[...].reshape(rows, D_MODEL)
    h = _rms_scale(x, g_ref[...]).astype(BF16)
    sxz = _dot(h, w_ref[...])
    sx = sxz[:, :BRANCH_W]
    sz = sxz[:, BRANCH_W:]
    ub = sx.astype(BF16)
    for k in range(2):
        o = _dot(ub[:, MXU_TILE * k:MXU_TILE * (k + 1)], wb_ref[k])
        bbr[:, half * k:half * (k + 1)] = o[:, :half]
        bbi[:, half * k:half * (k + 1)] = o[:, half:]

    cw = max(128, min(1024, (8 * 1024) // nb))
    for c0 in range(0, S5_NSTATE, cw):
        cs = slice(c0, c0 + cw)
        a_r = jnp.broadcast_to(ar_ref[:, cs], (nb, cw))
        a_i = jnp.broadcast_to(ai_ref[:, cs], (nb, cw))

        def step(t, carry, cs=cs, a_r=a_r, a_i=a_i):
            hr, hi = carry
            rs = pl.ds(pl.multiple_of(t * nb, nb), nb)
            nhr = a_r * hr - a_i * hi + bbr[rs, cs]
            nhi = a_r * hi + a_i * hr + bbi[rs, cs]
            bbr[rs, cs] = nhr
            bbi[rs, cs] = nhi
            return nhr, nhi

        hr, hi = lax.fori_loop(0, nt, step, (cr[:, cs], ci[:, cs]), unroll=min(nt, 8))
        cr[:, cs] = hr
        ci[:, cs] = hi

    ys = []
    for k in range(2):
        ks = slice(half * k, half * (k + 1))
        lhs = jnp.concatenate([bbr[:, ks].astype(BF16), bbi[:, ks].astype(BF16)], axis=1)
        ys.append(_dot(lhs, cc_ref[k]))
    y = jnp.concatenate(ys, axis=1) + d_ref[...] * sx
    y = _gelu_tanh(y)
    y = y * _sigmoid(_dot(y.astype(BF16), wglu_ref[...]))
    y_ref[...] = (y * _silu(sz)).astype(y_ref.dtype).reshape(y_ref.shape)
    nr_ref[...] = cr[...]
    ni_ref[...] = ci[...]


def _s5_call(x, wl, h0r, h0i, *, seq, batch, nt, nb, y_dtype):
    rows = nt * nb
    grid = (batch // nb, seq // nt)
    if nb == batch:
        x_spec = pl.BlockSpec((rows, D_MODEL), lambda i, j: (j, 0))
        y_spec = pl.BlockSpec((rows, BRANCH_W), lambda i, j: (j, 0))
        y_shape = jax.ShapeDtypeStruct((seq * batch, BRANCH_W), y_dtype)
    else:
        x_spec = pl.BlockSpec((nt, nb, D_MODEL), lambda i, j: (j, i, 0))
        y_spec = pl.BlockSpec((nt, nb, BRANCH_W), lambda i, j: (j, i, 0))
        y_shape = jax.ShapeDtypeStruct((seq, batch, BRANCH_W), y_dtype)
    st_spec = pl.BlockSpec((nb, S5_NSTATE), lambda i, j: (i, 0))
    st_shape = jax.ShapeDtypeStruct((batch, S5_NSTATE), F32)
    weights = (wl["g_pre"], wl["w_s5"], wl["s5_wb"], wl["s5_ar"], wl["s5_ai"], wl["s5_cc"], wl["s5_d"],
               wl["s5_wglu"])
    return pl.pallas_call(
        functools.partial(_s5_body, nt=nt, nb=nb),
        grid=grid,
        in_specs=[x_spec] + [_full_spec(w.shape) for w in weights] + [st_spec, st_spec],
        out_specs=[y_spec, st_spec, st_spec],
        out_shape=[y_shape, st_shape, st_shape],
        scratch_shapes=[pltpu.VMEM((rows, S5_NSTATE), F32), pltpu.VMEM((rows, S5_NSTATE), F32),
                        pltpu.VMEM((nb, S5_NSTATE), F32), pltpu.VMEM((nb, S5_NSTATE), F32)],
        compiler_params=_params(("arbitrary", "arbitrary")),
    )(x, *weights, h0r, h0i)


def _pool_body(x_ref, g_ref, w_ref, pw_ref, ps_ref, buf_ref, y_ref, nbuf_ref, ext, *, nt, nb, start):
    j = pl.program_id(1)
    rows = nt * nb
    prev = POOL_BUF * nb

    @pl.when(j == 0)
    def _():
        ext[0:prev, :] = buf_ref[...].reshape(prev, BRANCH_W)

    x = x_ref[...].reshape(rows, D_MODEL)
    h = _rms_scale(x, g_ref[...]).astype(BF16)
    pxz = _dot(h, w_ref[...])
    px = pxz[:, :BRANCH_W]
    pz = pxz[:, BRANCH_W:]
    ext[prev:prev + rows, :] = px

    t_local = lax.broadcasted_iota(jnp.int32, (rows, POOL_GW), 0) // nb
    pos1 = start + j * nt + t_local + 1
    outs = []
    for gi, win in enumerate(POOL_WINDOWS):
        cs = slice(POOL_GW * gi, POOL_GW * (gi + 1))
        acc = ext[prev:prev + rows, cs]
        for q in range(1, win):
            acc = acc + ext[prev - q * nb:prev - q * nb + rows, cs]
        cnt = jnp.minimum(pos1, win).astype(F32)
        outs.append(acc / cnt)
    pooled = jnp.concatenate(outs, axis=1) - px
    mixed = _dot(pooled.astype(BF16), pw_ref[...]) * ps_ref[...]
    y_ref[...] = (mixed * _silu(pz)).astype(y_ref.dtype).reshape(y_ref.shape)
    last = ext[rows:rows + prev, :]
    nbuf_ref[...] = last.reshape(nbuf_ref.shape)
    ext[0:prev, :] = last


def _pool_call(x, wl, buf, *, seq, batch, nt, nb, start, y_dtype):
    rows = nt * nb
    grid = (batch // nb, seq // nt)
    if nb == batch:
        x_spec = pl.BlockSpec((rows, D_MODEL), lambda i, j: (j, 0))
        y_spec = pl.BlockSpec((rows, BRANCH_W), lambda i, j: (j, 0))
        y_shape = jax.ShapeDtypeStruct((seq * batch, BRANCH_W), y_dtype)
        b_spec = pl.BlockSpec((POOL_BUF * nb, BRANCH_W), lambda i, j: (0, 0))
        b_shape = jax.ShapeDtypeStruct((POOL_BUF * batch, BRANCH_W), F32)
    else:
        x_spec = pl.BlockSpec((nt, nb, D_MODEL), lambda i, j: (j, i, 0))
        y_spec = pl.BlockSpec((nt, nb, BRANCH_W), lambda i, j: (j, i, 0))
        y_shape = jax.ShapeDtypeStruct((seq, batch, BRANCH_W), y_dtype)
        b_spec = pl.BlockSpec((POOL_BUF, nb, BRANCH_W), lambda i, j: (0, i, 0))
        b_shape = jax.ShapeDtypeStruct((POOL_BUF, batch, BRANCH_W), F32)
    weights = (wl["g_pre"], wl["w_pool"], wl["pool_w"], wl["pool_scale"])
    return pl.pallas_call(
        functools.partial(_pool_body, nt=nt, nb=nb, start=start),
        grid=grid,
        in_specs=[x_spec] + [_full_spec(w.shape) for w in weights] + [b_spec],
        out_specs=[y_spec, b_spec],
        out_shape=[y_shape, b_shape],
        scratch_shapes=[pltpu.VMEM(((POOL_BUF + nt) * nb, BRANCH_W), F32)],
        compiler_params=_params(("arbitrary", "arbitrary")),
    )(x, *weights, buf)


def _seg_sums(xs, ones):
    rows = xs[0].shape[0]
    stacked = jnp.concatenate([x.astype(BF16) for x in xs], axis=0)
    s = jnp.concatenate([_dot(stacked[:, MXU_TILE * g:MXU_TILE * (g + 1)], ones)
                         for g in range(BRANCH_W // MXU_TILE)], axis=1)
    return [s[i * rows:(i + 1) * rows] for i in range(len(xs))]


def _block_rows(y, head_masks):
    yb = y.astype(BF16)
    return jnp.concatenate([yb * m for m in head_masks], axis=0)


def _rw_body(x_ref, s0_ref, sh_ref, g_ref, w_ref, perm_ref, permt_ref, mu_ref, w0_ref, wl_ref, a0_ref,
             kkw_ref, kaw_ref, rkw_ref, gnw_ref, gnb_ref, ones_ref, tri_ref,
             y_ref, sout_ref, shout_ref,
             sbd, sbb, carry, qk_s, qr_s, kb_s, kk_s, v_s, ei_s, bonus_s, z_s, o_s,
             *, nt, nbb, part, gw):
    j = pl.program_id(1)
    rows = nt * nbb
    n_ch = rows // CHUNK
    spc = CHUNK // nt
    log_seq = nt.bit_length() - 1

    n_grp = BRANCH_W // gw
    grp_heads = gw // RWKV_HEAD

    lane = lax.broadcasted_iota(jnp.int32, (CHUNK, gw), 1)
    row64 = lax.broadcasted_iota(jnp.int32, (CHUNK, gw), 0)
    src = lane & (CHUNK - 1)
    same_seq = (row64 >> log_seq) == (src >> log_seq)
    m_strict = same_seq & (src < row64)
    m_incl = same_seq & (src <= row64)
    eye_sbs = (src == row64).astype(F32)
    head_masks = [((lane >> 6) == hh).astype(BF16) for hh in range(grp_heads)]
    half_lane = lax.broadcasted_iota(jnp.int32, (RWKV_HEAD, LANES), 1)
    half_masks = [half_lane < RWKV_HEAD, half_lane >= RWKV_HEAD]
    expand = ((lax.broadcasted_iota(jnp.int32, (RWKV_HEAD, gw), 1) & (RWKV_HEAD - 1))
              == lax.broadcasted_iota(jnp.int32, (RWKV_HEAD, gw), 0)).astype(BF16)

    @pl.when(j == 0)
    def _():
        brow = lax.broadcasted_iota(jnp.int32, (gw, gw), 0)
        bcol = lax.broadcasted_iota(jnp.int32, (gw, gw), 1)
        block_mask = (brow >> 6) == (bcol >> 6)
        for q in range(nbb):
            for g in range(n_grp):
                hi, mid, lo = _split3(s0_ref[q, gw * g:gw * (g + 1), :])
                wide = jnp.where(block_mask, _dot(hi, expand) + _dot(mid, expand) + _dot(lo, expand), 0.0)
                sbd[q, g] = wide
                sbb[q, g] = wide.astype(BF16)
        carry[...] = sh_ref[...]

    x = x_ref[...].reshape(rows, D_MODEL)
    h = _rms_scale(x, g_ref[...]).astype(BF16)
    hb = _dot(perm_ref[...], h).astype(BF16)
    ones = ones_ref[...]

    is_first = (lax.broadcasted_iota(jnp.int32, (part, RWKV_SHIFT_W), 0) & (nt - 1)) == 0
    lane128 = lax.broadcasted_iota(jnp.int32, (part, 2 * LORA_RANK), 1)
    for r0 in range(0, rows, part):
        rs = slice(r0, r0 + part)
        q0 = r0 // nt
        pz = _dot(hb[rs], w_ref[...])
        p = pz[:, :RWKV_SHIFT_W]
        z_s[rs, :] = pz[:, RWKV_SHIFT_W:]
        first = jnp.concatenate(
            [jnp.broadcast_to(carry[q0 + q:q0 + q + 1, :], (nt, RWKV_SHIFT_W)) for q in range(part // nt)], axis=0)
        prev = jnp.where(is_first, first, pltpu.roll(p, 1, 0))
        for q in range(part // nt):
            carry[q0 + q:q0 + q + 1, :] = p[nt * (q + 1) - 1:nt * (q + 1), :]
        pm = p + (prev - p) * mu_ref[...]
        r = pm[:, 0:BRANCH_W]
        k = pm[:, BRANCH_W:2 * BRANCH_W]
        v = pm[:, 2 * BRANCH_W:3 * BRANCH_W]
        ll = pm[:, 3 * BRANCH_W:]
        xl = jnp.where(lane128 < LORA_RANK, jnp.tanh(ll), ll).astype(BF16)
        dl = _dot(xl, wl_ref[...])
        w_log = -jax.nn.softplus(-(w0_ref[...] + dl[:, :BRANCH_W])) - 0.5
        ld = -jnp.exp(w_log)
        a = _sigmoid(a0_ref[...] + dl[:, BRANCH_W:])
        kk = k * kkw_ref[...]
        k2 = k * (1.0 + (a - 1.0) * kaw_ref[...])
        ssq, rk = _seg_sums([kk * kk, r * k2 * rkw_ref[...]], ones)
        kkn = kk * lax.rsqrt(jnp.maximum(ssq, 1e-24))
        hi, mid, lo = _split3(ld)
        cum3 = _dot(tri_ref[...], jnp.concatenate([hi, mid, lo], axis=1))
        cum = cum3[:, :BRANCH_W] + cum3[:, BRANCH_W:2 * BRANCH_W] + cum3[:, 2 * BRANCH_W:]
        e_incl = jnp.exp(cum)
        e_inv = jnp.exp(-cum)
        qk_s[rs, :] = (kkn * jnp.exp(cum - ld)).astype(BF16)
        qr_s[rs, :] = r * e_incl
        kb_s[rs, :] = (kkn * a * e_inv).astype(BF16)
        kk_s[rs, :] = (k2 * e_inv).astype(BF16)
        v_s[rs, :] = v.astype(BF16)
        ei_s[rs, :] = e_incl
        bonus_s[rs, :] = rk * v

    def chunk(c, g):
        r0 = c * CHUNK if isinstance(c, int) else pl.multiple_of(c * CHUNK, CHUNK)
        rs = pl.ds(r0, CHUNK)
        ls = slice(gw * g, gw * (g + 1))
        q_k = qk_s[rs, ls]
        q_r = qr_s[rs, ls]
        k_b = kb_s[rs, ls]
        k_k = kk_s[rs, ls]
        vb = v_s[rs, ls]
        q2 = jnp.concatenate([q_k, q_r.astype(BF16)], axis=0)
        bd_v = _block_rows(vb, head_masks)
        a_all = _dot_nt(q2, jnp.concatenate([_block_rows(k_b, head_masks), _block_rows(k_k, head_masks)], axis=0))
        a_kb = jnp.where(m_strict, a_all[:CHUNK, :gw], 0.0)
        a_rb = jnp.where(m_incl, a_all[CHUNK:, :gw], 0.0).astype(BF16)
        a_kk = jnp.where(m_strict, a_all[:CHUNK, gw:], 0.0).astype(BF16)
        a_rk = jnp.where(m_incl, a_all[CHUNK:, gw:], 0.0).astype(BF16)
        av = _dot(jnp.concatenate([a_kk, a_rk], axis=0), bd_v)
        akv = av[:CHUNK]
        o_v = av[CHUNK:]
        npow = -a_kb
        tinv = eye_sbs + npow
        npow = _dot(npow.astype(BF16), _block_rows(npow, head_masks))
        for it in range(1, log_seq):
            bd_n = _block_rows(npow, head_masks)
            if it + 1 < log_seq:
                both = _dot(jnp.concatenate([npow.astype(BF16), tinv.astype(BF16)], axis=0), bd_n)
                npow = both[:CHUNK]
                tinv = tinv + both[CHUNK:]
            else:
                tinv = tinv + _dot(tinv.astype(BF16), bd_n)
        tw = _dot(tinv.astype(BF16),
                  jnp.concatenate([_block_rows(q_k, head_masks), _block_rows(-akv, head_masks)], axis=1))
        qt_k = tw[:, :gw]
        w_loc = tw[:, gw:]
        d1 = _dot(a_rb, jnp.concatenate([_block_rows(qt_k, head_masks), _block_rows(w_loc, head_masks)], axis=1))
        qh_r = q_r - d1[:, :gw]
        o_loc = d1[:, gw:] + o_v

        def update_state(sq, zz, pc):
            for hh in range(grp_heads):
                hr = slice(RWKV_HEAD * hh, RWKV_HEAD * (hh + 1))
                hl = slice(LANES * (hh // 2), LANES * (hh // 2 + 1))
                piece = (sbd[sq, g, hr, hl] + jnp.where(half_masks[hh % 2], zz[hr, hl], 0.0)) * pc[:, hl]
                sbd[sq, g, hr, hl] = piece
                sbb[sq, g, hr, hl] = piece.astype(BF16)

        if spc == 1:
            xs = _dot_nt(jnp.concatenate([qt_k.astype(BF16), qh_r.astype(BF16)], axis=0), sbb[c, g])
            u = w_loc - xs[:CHUNK]
            o_s[rs, ls] = xs[CHUNK:] + o_loc
            zz = _dot_tn(jnp.concatenate([u.astype(BF16), vb], axis=0), jnp.concatenate([k_b, k_k], axis=0))
            update_state(c, zz, ei_s[pl.ds(r0 + CHUNK - 1, 1), ls])
        else:
            kb_f = k_b.astype(F32)
            kk_f = k_k.astype(F32)
            v_f = vb.astype(F32)
            o_rows = []
            for q in range(spc):
                qs = slice(nt * q, nt * (q + 1))
                sq = c * spc + q
                xs = _dot_nt(jnp.concatenate([qt_k[qs], qh_r[qs]], axis=0).astype(BF16), sbb[sq, g])
                u = w_loc[qs] - xs[:nt]
                o_rows.append(xs[nt:] + o_loc[qs])
                zz = _dot_tn(jnp.concatenate([u, v_f[qs]], axis=0).astype(BF16),
                             jnp.concatenate([kb_f[qs], kk_f[qs]], axis=0).astype(BF16))
                update_state(sq, zz, ei_s[pl.ds(r0 + nt * (q + 1) - 1, 1), ls])
            o_s[rs, ls] = jnp.concatenate(o_rows, axis=0)

    if n_ch <= RW_UNROLL:
        for c in range(n_ch):
            for g in range(n_grp):
                chunk(c, g)
    else:
        def loop_body(cc, carry_):
            for i in range(RW_UNROLL):
                for g in range(n_grp):
                    chunk(cc * RW_UNROLL + i, g)
            return carry_
        lax.fori_loop(0, n_ch // RW_UNROLL, loop_body, 0)

    ybs = []
    for r0 in range(0, rows, part):
        rs = slice(r0, r0 + part)
        o = o_s[rs, :]
        mean = _seg_sums([o], ones)[0] * (1.0 / RWKV_HEAD)
        dev = o - mean
        var = _seg_sums([dev * dev], ones)[0] * (1.0 / RWKV_HEAD)
        o = dev * lax.rsqrt(var + GN_EPS) * gnw_ref[...] + gnb_ref[...] + bonus_s[rs, :]
        ybs.append((o * _silu(z_s[rs, :])).astype(BF16))
    y_tm = _dot(permt_ref[...], jnp.concatenate(ybs, axis=0))
    y_ref[...] = y_tm.astype(y_ref.dtype).reshape(y_ref.shape)
    shout_ref[...] = carry[...]

    @pl.when(j == pl.num_programs(1) - 1)
    def _():
        for q in range(nbb):
            for g in range(n_grp):
                hi, mid, lo = _split3(sbd[q, g])
                sout_ref[q, gw * g:gw * (g + 1), :] = (_dot_nt(hi, expand) + _dot_nt(mid, expand)
                                                       + _dot_nt(lo, expand))


def _rw_constants(nt, nbb, part):
    rows = nt * nbb
    r = np.arange(rows)
    src = (r % nt) * nbb + (r // nt)
    perm = np.zeros((rows, rows), np.float32)
    perm[r, src] = 1.0
    t = np.arange(part)
    tri = ((t[:, None] // nt) == (t[None, :] // nt)) & (t[None, :] <= t[:, None])
    return jnp.asarray(perm, BF16), jnp.asarray(perm.T, BF16), jnp.asarray(tri, BF16)


def _rw_call(x, shift0, s0, wl, *, seq, batch, nt, nbb, y_dtype):
    rows = nt * nbb
    part = min(rows, 256)
    gw = RW_GROUP if nt == CHUNK else MXU_TILE
    grid = (batch // nbb, seq // nt)
    if nbb == batch:
        x_spec = pl.BlockSpec((rows, D_MODEL), lambda i, j: (j, 0))
        y_spec = pl.BlockSpec((rows, BRANCH_W), lambda i, j: (j, 0))
        y_shape = jax.ShapeDtypeStruct((seq * batch, BRANCH_W), y_dtype)
    else:
        x_spec = pl.BlockSpec((nt, nbb, D_MODEL), lambda i, j: (j, i, 0))
        y_spec = pl.BlockSpec((nt, nbb, BRANCH_W), lambda i, j: (j, i, 0))
        y_shape = jax.ShapeDtypeStruct((seq, batch, BRANCH_W), y_dtype)
    st_spec = pl.BlockSpec((nbb, BRANCH_W, RWKV_HEAD), lambda i, j: (i, 0, 0))
    sh_spec = pl.BlockSpec((nbb, RWKV_SHIFT_W), lambda i, j: (i, 0))
    perm, permt, tri = _rw_constants(nt, nbb, part)
    weights = (wl["g_pre"], wl["w_rw"], perm, permt, wl["rw_mu"], wl["rw_w0"], wl["rw_lora"], wl["rw_a0"],
               wl["rw_kk"], wl["rw_ka"], wl["rw_rk"], wl["rw_gnw"], wl["rw_gnb"], wl["rw_ones"], tri)
    work = lambda dt: pltpu.VMEM((rows, BRANCH_W), dt)
    return pl.pallas_call(
        functools.partial(_rw_body, nt=nt, nbb=nbb, part=part, gw=gw),
        grid=grid,
        in_specs=[x_spec, st_spec, sh_spec] + [_full_spec(w.shape) for w in weights],
        out_specs=[y_spec, st_spec, sh_spec],
        out_shape=[y_shape, jax.ShapeDtypeStruct((batch, BRANCH_W, RWKV_HEAD), F32),
                   jax.ShapeDtypeStruct((batch, RWKV_SHIFT_W), F32)],
        scratch_shapes=[pltpu.VMEM((nbb, BRANCH_W // gw, gw, gw), F32),
                        pltpu.VMEM((nbb, BRANCH_W // gw, gw, gw), BF16),
                        pltpu.VMEM((nbb, RWKV_SHIFT_W), F32),
                        work(BF16), work(F32), work(BF16), work(BF16), work(BF16), work(F32), work(F32),
                        work(F32), work(F32)],
        compiler_params=_params(("arbitrary", "arbitrary")),
    )(x, s0, shift0, *weights)


def _merge_body(x_ref, ys5_ref, yrw_ref, ypl_ref, gpre_ref, wg_ref, wu_ref, wo_ref, gpost_ref, o_ref):
    x = x_ref[...]
    h = _rms_scale(x, gpre_ref[...]).astype(BF16)
    merged = None
    for i, y_ref in enumerate((ys5_ref, yrw_ref, ypl_ref)):
        gate = _sigmoid(_dot(h, wg_ref[:, D_MODEL * i:D_MODEL * (i + 1)]))
        term = gate * _dot(y_ref[...].astype(BF16), wu_ref[i])
        merged = term if merged is None else merged + term
    out = _dot(merged.astype(BF16), wo_ref[...])
    o_ref[...] = x + _rms_scale(out, gpost_ref[...])


def _merge_call(x, ys5, yrw, ypl, wl, *, tm):
    m = x.shape[0]
    weights = (wl["g_pre"], wl["w_gate"], wl["w_up"], wl["w_o"], wl["g_post"])
    row_spec = lambda w: pl.BlockSpec((tm, w), lambda i: (i, 0))
    return pl.pallas_call(
        _merge_body,
        grid=(m // tm,),
        in_specs=[row_spec(D_MODEL), row_spec(BRANCH_W), row_spec(BRANCH_W), row_spec(BRANCH_W)]
        + [_full_spec(w.shape) for w in weights],
        out_specs=row_spec(D_MODEL),
        out_shape=jax.ShapeDtypeStruct((m, D_MODEL), F32),
        compiler_params=_params(("parallel",)),
    )(x, ys5, yrw, ypl, *weights)


def _block_diag(blocks):
    n, r, c = blocks.shape
    eye = jnp.eye(n, dtype=blocks.dtype)
    return jnp.einsum("grc,gh->grhc", blocks, eye).reshape(n * r, n * c)


def _prep_layer(l, norm_pre, norm_post, w_in, s5_lam_re, s5_lam_im, s5_log_dt, s5_b_re, s5_b_im,
                s5_c_re, s5_c_im, s5_d, s5_w_glu, rwkv_mu, rwkv_w0, rwkv_w_w2, rwkv_a0, rwkv_w_a2,
                rwkv_k_k, rwkv_k_a, rwkv_r_k, rwkv_gn_w, rwkv_gn_b, pool_w, pool_scale,
                w_up_s5, w_up_rwkv, w_up_pool, w_o):
    row = lambda a: a.astype(F32).reshape(1, -1)
    wl = {}
    wl["g_pre"] = row(norm_pre[l])
    wl["g_post"] = row(norm_post[l])
    win = w_in[l].astype(BF16)
    wl["w_s5"] = win[:, 0:1024]
    wl["w_rw"] = win[:, 1024:3200]
    wl["w_pool"] = win[:, 3200:4224]
    wl["w_gate"] = win[:, 4224:7296]

    lr = s5_lam_re[l].astype(F32)
    li = s5_lam_im[l].astype(F32)
    dt = jnp.exp(s5_log_dt[l].astype(F32))[:, None]
    mag = jnp.exp(lr * dt)
    ab_r = mag * jnp.cos(li * dt)
    ab_i = mag * jnp.sin(li * dt)
    den = lr * lr + li * li
    nr = ab_r - 1.0
    co_r = ((nr * lr + ab_i * li) / den)[:, :, None]
    co_i = ((ab_i * lr - nr * li) / den)[:, :, None]
    b_re = s5_b_re[l].astype(F32)
    b_im = s5_b_im[l].astype(F32)
    bf_re = co_r * b_re - co_i * b_im
    bf_im = co_r * b_im + co_i * b_re
    gh = S5_GROUPS // 2
    wb = []
    for k in range(2):
        gs = slice(gh * k, gh * (k + 1))
        wb.append(jnp.concatenate([_block_diag(jnp.swapaxes(bf_re[gs], 1, 2)),
                                   _block_diag(jnp.swapaxes(bf_im[gs], 1, 2))], axis=1))
    wl["s5_wb"] = jnp.stack(wb).astype(BF16)
    wl["s5_ar"] = ab_r.reshape(1, S5_NSTATE)
    wl["s5_ai"] = ab_i.reshape(1, S5_NSTATE)
    c_re = s5_c_re[l].astype(F32)
    c_im = s5_c_im[l].astype(F32)
    cc = []
    for k in range(2):
        gs = slice(gh * k, gh * (k + 1))
        cc.append(jnp.concatenate([_block_diag(jnp.swapaxes(c_re[gs], 1, 2)),
                                   _block_diag(jnp.swapaxes(-c_im[gs], 1, 2))], axis=0))
    wl["s5_cc"] = jnp.stack(cc).astype(BF16)
    wl["s5_d"] = row(s5_d[l])
    wl["s5_wglu"] = s5_w_glu[l].astype(BF16)

    wl["rw_mu"] = row(rwkv_mu[l])
    wl["rw_w0"] = row(rwkv_w0[l])
    zeros = jnp.zeros((LORA_RANK, BRANCH_W), F32)
    wl["rw_lora"] = jnp.concatenate(
        [jnp.concatenate([rwkv_w_w2[l].astype(F32), zeros], axis=1),
         jnp.concatenate([zeros, rwkv_w_a2[l].astype(F32)], axis=1)], axis=0).astype(BF16)
    wl["rw_a0"] = row(rwkv_a0[l])
    wl["rw_kk"] = row(rwkv_k_k[l])
    wl["rw_ka"] = row(rwkv_k_a[l])
    wl["rw_rk"] = row(rwkv_r_k[l])
    wl["rw_gnw"] = row(rwkv_gn_w[l])
    wl["rw_gnb"] = row(rwkv_gn_b[l])
    wl["rw_ones"] = _block_diag(jnp.ones((MXU_TILE // RWKV_HEAD, RWKV_HEAD, RWKV_HEAD), F32)).astype(BF16)

    wl["pool_w"] = _block_diag(pool_w[l].astype(F32)).astype(BF16)
    wl["pool_scale"] = row(pool_scale[l])
    wl["w_up"] = jnp.stack([w_up_s5[l], w_up_rwkv[l], w_up_pool[l]]).astype(BF16)
    wl["w_o"] = w_o[l].astype(BF16)
    return wl


def _layer(x, states, wl, *, seq, batch, start, prompt):
    s5r, s5i, rws, rwsh, pbuf = states
    if prompt:
        nt, nb, nbb, y_dtype = 64, batch, batch, BF16
    else:
        nt, nb, nbb, y_dtype = seq, min(batch, 32), min(batch, 16), F32
    x3 = x.reshape(seq, batch, D_MODEL)
    flat = lambda y: y.reshape(seq * batch, BRANCH_W)
    if nb == batch:
        x_tm = x
        pbuf_in = pbuf.reshape(POOL_BUF * batch, BRANCH_W)
    else:
        x_tm = x3
        pbuf_in = pbuf
    ys5, n_re, n_im = _s5_call(x_tm, wl, s5r, s5i, seq=seq, batch=batch, nt=nt, nb=nb, y_dtype=y_dtype)
    ypl, n_buf = _pool_call(x_tm, wl, pbuf_in, seq=seq, batch=batch, nt=nt, nb=nb, start=start, y_dtype=y_dtype)
    yrw, n_s, n_shift = _rw_call(x if nbb == batch else x3, rwsh, rws, wl, seq=seq, batch=batch, nt=nt, nbb=nbb,
                                 y_dtype=y_dtype)
    x_new = _merge_call(x, flat(ys5), flat(yrw), flat(ypl), wl, tm=512)
    n_buf = n_buf.reshape(POOL_BUF, batch, BRANCH_W)
    return x_new, (n_re, n_im, n_s, n_shift, n_buf)


def kernel(x_prompt, x_sample, state_s5_re, state_s5_im, state_rwkv, state_shift, state_pool, norm_pre, norm_post, w_in, s5_lam_re, s5_lam_im, s5_log_dt, s5_b_re, s5_b_im, s5_c_re, s5_c_im, s5_d, s5_w_glu, rwkv_mu, rwkv_w0, rwkv_w_w2, rwkv_a0, rwkv_w_a2, rwkv_k_k, rwkv_k_a, rwkv_r_k, rwkv_gn_w, rwkv_gn_b, pool_w, pool_scale, w_up_s5, w_up_rwkv, w_up_pool, w_o):
    depth = w_in.shape[0]
    bp, tp, _ = x_prompt.shape
    bs, ts, _ = x_sample.shape
    xp = jnp.transpose(x_prompt, (1, 0, 2)).reshape(tp * bp, D_MODEL)
    xs = jnp.transpose(x_sample, (1, 0, 2)).reshape(ts * bs, D_MODEL)
    outs_p = [[] for _ in range(5)]
    outs_s = [[] for _ in range(5)]
    for l in range(depth):
        wl = _prep_layer(l, norm_pre, norm_post, w_in, s5_lam_re, s5_lam_im, s5_log_dt, s5_b_re, s5_b_im,
                         s5_c_re, s5_c_im, s5_d, s5_w_glu, rwkv_mu, rwkv_w0, rwkv_w_w2, rwkv_a0, rwkv_w_a2,
                         rwkv_k_k, rwkv_k_a, rwkv_r_k, rwkv_gn_w, rwkv_gn_b, pool_w, pool_scale,
                         w_up_s5, w_up_rwkv, w_up_pool, w_o)
        zeros_p = (jnp.zeros((bp, S5_NSTATE), F32), jnp.zeros((bp, S5_NSTATE), F32),
                   jnp.zeros((bp, BRANCH_W, RWKV_HEAD), F32), jnp.zeros((bp, RWKV_SHIFT_W), F32),
                   jnp.zeros((POOL_BUF, bp, BRANCH_W), F32))
        xp, new_p = _layer(xp, zeros_p, wl, seq=tp, batch=bp, start=0, prompt=True)
        st_s = (state_s5_re[l].reshape(bs, S5_NSTATE), state_s5_im[l].reshape(bs, S5_NSTATE),
                state_rwkv[l].reshape(bs, BRANCH_W, RWKV_HEAD), state_shift[l],
                jnp.transpose(state_pool[l], (1, 0, 2)))
        xs, new_s = _layer(xs, st_s, wl, seq=ts, batch=bs, start=PAST_LEN, prompt=False)
        for lst, val in zip(outs_p, new_p):
            lst.append(val)
        for lst, val in zip(outs_s, new_s):
            lst.append(val)

    def finish(outs, b):
        re, im, rw, sh, pb = [jnp.stack(v) for v in outs]
        return (re.reshape(depth, b, S5_GROUPS, S5_STATE), im.reshape(depth, b, S5_GROUPS, S5_STATE),
                rw.reshape(depth, b, RWKV_HEADS, RWKV_HEAD, RWKV_HEAD), sh,
                jnp.transpose(pb, (0, 2, 1, 3)))

    y_p = jnp.transpose(xp.reshape(tp, bp, D_MODEL), (1, 0, 2))
    y_s = jnp.transpose(xs.reshape(ts, bs, D_MODEL), (1, 0, 2))
    return (y_p, y_s) + finish(outs_p, bp) + finish(outs_s, bs)
```

```python
import functools
import math

import jax
import jax.numpy as jnp
import numpy as np
from jax import lax
from jax.experimental import pallas as pl
from jax.experimental.pallas import tpu as pltpu

F32 = jnp.float32
BF16 = jnp.bfloat16

D_MODEL = 1024
BRANCH_W = 512
S5_GROUPS = 32
S5_GROUP = 16
S5_STATE = 64
S5_NSTATE = S5_GROUPS * S5_STATE
RWKV_HEADS = 8
RWKV_HEAD = 64
LORA_RANK = 64
RWKV_SHIFT_W = 3 * BRANCH_W + 2 * LORA_RANK
POOL_WINDOWS = (2, 4, 8, 16)
POOL_GW = 128
POOL_BUF = 15
NORM_EPS = 1e-6
GN_EPS = 64e-5
PAST_LEN = 16384

MXU_TILE = 256
LANES = 128
CHUNK = 64
RW_GROUP = 256
RW_UNROLL = 8
VMEM_LIMIT = 56 * 1024 * 1024


def _dot(a, b):
    return jnp.dot(a, b, preferred_element_type=F32)


def _dot_nt(a, b):
    return lax.dot_general(a, b, (((1,), (1,)), ((), ())), preferred_element_type=F32)


def _dot_tn(a, b):
    return lax.dot_general(a, b, (((0,), (0,)), ((), ())), preferred_element_type=F32)


def _split3(x):
    hi = x.astype(BF16)
    r1 = x - hi.astype(F32)
    mid = r1.astype(BF16)
    lo = (r1 - mid.astype(F32)).astype(BF16)
    return hi, mid, lo


def _rms_scale(x, g):
    ms = jnp.mean(x * x, axis=-1, keepdims=True)
    return x * lax.rsqrt(ms + NORM_EPS) * g


def _sigmoid(x):
    return jax.nn.sigmoid(x)


def _silu(x):
    return x * jax.nn.sigmoid(x)


def _gelu_tanh(x):
    c = math.sqrt(2.0 / math.pi)
    return 0.5 * x * (1.0 + jnp.tanh(c * (x + 0.044715 * (x * x * x))))


def _full_spec(shape):
    nd = len(shape)
    return pl.BlockSpec(shape, lambda *_: (0,) * nd)


def _params(sem):
    return pltpu.CompilerParams(dimension_semantics=sem, vmem_limit_bytes=VMEM_LIMIT)


def _s5_body(x_ref, g_ref, w_ref, wb_ref, ar_ref, ai_ref, cc_ref, d_ref, wglu_ref, h0r_ref, h0i_ref,
             y_ref, nr_ref, ni_ref, bbr, bbi, cr, ci, *, nt, nb):
    j = pl.program_id(1)
    rows = nt * nb
    half = S5_NSTATE // 2

    @pl.when(j == 0)
    def _():
        cr[...] = h0r_ref[...]
        ci[...] = h0i_ref[...]

    x = x_ref[...].reshape(rows, D_MODEL)
    h = _rms_scale(x, g_ref[...]).astype(BF16)
    sxz = _dot(h, w_ref[...])
    sx = sxz[:, :BRANCH_W]
    sz = sxz[:, BRANCH_W:]
    ub = sx.astype(BF16)
    for k in range(2):
        o = _dot(ub[:, MXU_TILE * k:MXU_TILE * (k + 1)], wb_ref[k])
        bbr[:, half * k:half * (k + 1)] = o[:, :half]
        bbi[:, half * k:half * (k + 1)] = o[:, half:]

    cw = max(128, min(1024, (8 * 1024) // nb))
    for c0 in range(0, S5_NSTATE, cw):
        cs = slice(c0, c0 + cw)
        a_r = jnp.broadcast_to(ar_ref[:, cs], (nb, cw))
        a_i = jnp.broadcast_to(ai_ref[:, cs], (nb, cw))

        def step(t, carry, cs=cs, a_r=a_r, a_i=a_i):
            hr, hi = carry
            rs = pl.ds(pl.multiple_of(t * nb, nb), nb)
            nhr = a_r * hr - a_i * hi + bbr[rs, cs]
            nhi = a_r * hi + a_i * hr + bbi[rs, cs]
            bbr[rs, cs] = nhr
            bbi[rs, cs] = nhi
            return nhr, nhi

        hr, hi = lax.fori_loop(0, nt, step, (cr[:, cs], ci[:, cs]), unroll=min(nt, 8))
        cr[:, cs] = hr
        ci[:, cs] = hi

    ys = []
    for k in range(2):
        ks = slice(half * k, half * (k + 1))
        lhs = jnp.concatenate([bbr[:, ks].astype(BF16), bbi[:, ks].astype(BF16)], axis=1)
        ys.append(_dot(lhs, cc_ref[k]))
    y = jnp.concatenate(ys, axis=1) + d_ref[...] * sx
    y = _gelu_tanh(y)
    y = y * _sigmoid(_dot(y.astype(BF16), wglu_ref[...]))
    y_ref[...] = (y * _silu(sz)).astype(y_ref.dtype).reshape(y_ref.shape)
    nr_ref[...] = cr[...]
    ni_ref[...] = ci[...]


def _s5_call(x, wl, h0r, h0i, *, seq, batch, nt, nb, y_dtype):
    rows = nt * nb
    grid = (batch // nb, seq // nt)
    if nb == batch:
        x_spec = pl.BlockSpec((rows, D_MODEL), lambda i, j: (j, 0))
        y_spec = pl.BlockSpec((rows, BRANCH_W), lambda i, j: (j, 0))
        y_shape = jax.ShapeDtypeStruct((seq * batch, BRANCH_W), y_dtype)
    else:
        x_spec = pl.BlockSpec((nt, nb, D_MODEL), lambda i, j: (j, i, 0))
        y_spec = pl.BlockSpec((nt, nb, BRANCH_W), lambda i, j: (j, i, 0))
        y_shape = jax.ShapeDtypeStruct((seq, batch, BRANCH_W), y_dtype)
    st_spec = pl.BlockSpec((nb, S5_NSTATE), lambda i, j: (i, 0))
    st_shape = jax.ShapeDtypeStruct((batch, S5_NSTATE), F32)
    weights = (wl["g_pre"], wl["w_s5"], wl["s5_wb"], wl["s5_ar"], wl["s5_ai"], wl["s5_cc"], wl["s5_d"],
               wl["s5_wglu"])
    return pl.pallas_call(
        functools.partial(_s5_body, nt=nt, nb=nb),
        grid=grid,
        in_specs=[x_spec] + [_full_spec(w.shape) for w in weights] + [st_spec, st_spec],
        out_specs=[y_spec, st_spec, st_spec],
        out_shape=[y_shape, st_shape, st_shape],
        scratch_shapes=[pltpu.VMEM((rows, S5_NSTATE), F32), pltpu.VMEM((rows, S5_NSTATE), F32),
                        pltpu.VMEM((nb, S5_NSTATE), F32), pltpu.VMEM((nb, S5_NSTATE), F32)],
        compiler_params=_params(("arbitrary", "arbitrary")),
    )(x, *weights, h0r, h0i)


def _pool_body(x_ref, g_ref, w_ref, pw_ref, ps_ref, buf_ref, y_ref, nbuf_ref, ext, *, nt, nb, start):
    j = pl.program_id(1)
    rows = nt * nb
    prev = POOL_BUF * nb

    @pl.when(j == 0)
    def _():
        ext[0:prev, :] = buf_ref[...].reshape(prev, BRANCH_W)

    x = x_ref[...].reshape(rows, D_MODEL)
    h = _rms_scale(x, g_ref[...]).astype(BF16)
    pxz = _dot(h, w_ref[...])
    px = pxz[:, :BRANCH_W]
    pz = pxz[:, BRANCH_W:]
    ext[prev:prev + rows, :] = px

    t_local = lax.broadcasted_iota(jnp.int32, (rows, POOL_GW), 0) // nb
    pos1 = start + j * nt + t_local + 1
    outs = []
    for gi, win in enumerate(POOL_WINDOWS):
        cs = slice(POOL_GW * gi, POOL_GW * (gi + 1))
        acc = ext[prev:prev + rows, cs]
        for q in range(1, win):
            acc = acc + ext[prev - q * nb:prev - q * nb + rows, cs]
        cnt = jnp.minimum(pos1, win).astype(F32)
        outs.append(acc / cnt)
    pooled = jnp.concatenate(outs, axis=1) - px
    mixed = _dot(pooled.astype(BF16), pw_ref[...]) * ps_ref[...]
    y_ref[...] = (mixed * _silu(pz)).astype(y_ref.dtype).reshape(y_ref.shape)
    last = ext[rows:rows + prev, :]
    nbuf_ref[...] = last.reshape(nbuf_ref.shape)
    ext[0:prev, :] = last


def _pool_call(x, wl, buf, *, seq, batch, nt, nb, start, y_dtype):
    rows = nt * nb
    grid = (batch // nb, seq // nt)
    if nb == batch:
        x_spec = pl.BlockSpec((rows, D_MODEL), lambda i, j: (j, 0))
        y_spec = pl.BlockSpec((rows, BRANCH_W), lambda i, j: (j, 0))
        y_shape = jax.ShapeDtypeStruct((seq * batch, BRANCH_W), y_dtype)
        b_spec = pl.BlockSpec((POOL_BUF * nb, BRANCH_W), lambda i, j: (0, 0))
        b_shape = jax.ShapeDtypeStruct((POOL_BUF * batch, BRANCH_W), F32)
    else:
        x_spec = pl.BlockSpec((nt, nb, D_MODEL), lambda i, j: (j, i, 0))
        y_spec = pl.BlockSpec((nt, nb, BRANCH_W), lambda i, j: (j, i, 0))
        y_shape = jax.ShapeDtypeStruct((seq, batch, BRANCH_W), y_dtype)
        b_spec = pl.BlockSpec((POOL_BUF, nb, BRANCH_W), lambda i, j: (0, i, 0))
        b_shape = jax.ShapeDtypeStruct((POOL_BUF, batch, BRANCH_W), F32)
    weights = (wl["g_pre"], wl["w_pool"], wl["pool_w"], wl["pool_scale"])
    return pl.pallas_call(
        functools.partial(_pool_body, nt=nt, nb=nb, start=start),
        grid=grid,
        in_specs=[x_spec] + [_full_spec(w.shape) for w in weights] + [b_spec],
        out_specs=[y_spec, b_spec],
        out_shape=[y_shape, b_shape],
        scratch_shapes=[pltpu.VMEM(((POOL_BUF + nt) * nb, BRANCH_W), F32)],
        compiler_params=_params(("arbitrary", "arbitrary")),
    )(x, *weights, buf)


def _seg_sums(xs, ones):
    rows = xs[0].shape[0]
    stacked = jnp.concatenate([x.astype(BF16) for x in xs], axis=0)
    s = jnp.concatenate([_dot(stacked[:, MXU_TILE * g:MXU_TILE * (g + 1)], ones)
                         for g in range(BRANCH_W // MXU_TILE)], axis=1)
    return [s[i * rows:(i + 1) * rows] for i in range(len(xs))]


def _block_rows(y, head_masks):
    yb = y.astype(BF16)
    return jnp.concatenate([yb * m for m in head_masks], axis=0)


def _rw_body(x_ref, s0_ref, sh_ref, g_ref, w_ref, perm_ref, permt_ref, mu_ref, w0_ref, wl_ref, a0_ref,
             kkw_ref, kaw_ref, rkw_ref, gnw_ref, gnb_ref, ones_ref, tri_ref,
             y_ref, sout_ref, shout_ref,
             sbd, sbb, carry, qk_s, qr_s, kb_s, kk_s, v_s, ei_s, bonus_s, z_s, o_s,
             *, nt, nbb, part, gw):
    j = pl.program_id(1)
    rows = nt * nbb
    n_ch = rows // CHUNK
    spc = CHUNK // nt
    log_seq = nt.bit_length() - 1

    n_grp = BRANCH_W // gw
    grp_heads = gw // RWKV_HEAD

    lane = lax.broadcasted_iota(jnp.int32, (CHUNK, gw), 1)
    row64 = lax.broadcasted_iota(jnp.int32, (CHUNK, gw), 0)
    src = lane & (CHUNK - 1)
    same_seq = (row64 >> log_seq) == (src >> log_seq)
    m_strict = same_seq & (src < row64)
    m_incl = same_seq & (src <= row64)
    eye_sbs = (src == row64).astype(F32)
    head_masks = [((lane >> 6) == hh).astype(BF16) for hh in range(grp_heads)]
    half_lane = lax.broadcasted_iota(jnp.int32, (RWKV_HEAD, LANES), 1)
    half_masks = [half_lane < RWKV_HEAD, half_lane >= RWKV_HEAD]

    @pl.when(j == 0)
    def _():
        zero_blk = jnp.zeros((RWKV_HEAD, RWKV_HEAD), F32)
        for q in range(nbb):
            for g in range(n_grp):
                sbd[q, g] = jnp.zeros((gw, gw), F32)
                sbb[q, g] = jnp.zeros((gw, gw), BF16)
                for hh in range(grp_heads):
                    hr = slice(RWKV_HEAD * hh, RWKV_HEAD * (hh + 1))
                    hl = slice(LANES * (hh // 2), LANES * (hh // 2 + 1))
                    blk = s0_ref[q, gw * g + RWKV_HEAD * hh:gw * g + RWKV_HEAD * (hh + 1), :]
                    piece = jnp.concatenate([blk, zero_blk] if hh % 2 == 0 else [zero_blk, blk], axis=1)
                    sbd[q, g, hr, hl] = piece
                    sbb[q, g, hr, hl] = piece.astype(BF16)
        carry[...] = sh_ref[...]

    x = x_ref[...].reshape(rows, D_MODEL)
    h = _rms_scale(x, g_ref[...]).astype(BF16)
    hb = _dot(perm_ref[...], h).astype(BF16)
    ones = ones_ref[...]

    is_first = (lax.broadcasted_iota(jnp.int32, (part, RWKV_SHIFT_W), 0) & (nt - 1)) == 0
    lane128 = lax.broadcasted_iota(jnp.int32, (part, 2 * LORA_RANK), 1)
    for r0 in range(0, rows, part):
        rs = slice(r0, r0 + part)
        q0 = r0 // nt
        pz = _dot(hb[rs], w_ref[...])
        p = pz[:, :RWKV_SHIFT_W]
        z_s[rs, :] = pz[:, RWKV_SHIFT_W:]
        first = jnp.concatenate(
            [jnp.broadcast_to(carry[q0 + q:q0 + q + 1, :], (nt, RWKV_SHIFT_W)) for q in range(part // nt)], axis=0)
        prev = jnp.where(is_first, first, pltpu.roll(p, 1, 0))
        for q in range(part // nt):
            carry[q0 + q:q0 + q + 1, :] = p[nt * (q + 1) - 1:nt * (q + 1), :]
        pm = p + (prev - p) * mu_ref[...]
        r = pm[:, 0:BRANCH_W]
        k = pm[:, BRANCH_W:2 * BRANCH_W]
        v = pm[:, 2 * BRANCH_W:3 * BRANCH_W]
        ll = pm[:, 3 * BRANCH_W:]
        xl = jnp.where(lane128 < LORA_RANK, jnp.tanh(ll), ll).astype(BF16)
        dl = _dot(xl, wl_ref[...])
        w_log = -jax.nn.softplus(-(w0_ref[...] + dl[:, :BRANCH_W])) - 0.5
        ld = -jnp.exp(w_log)
        a = _sigmoid(a0_ref[...] + dl[:, BRANCH_W:])
        kk = k * kkw_ref[...]
        k2 = k * (1.0 + (a - 1.0) * kaw_ref[...])
        ssq, rk = _seg_sums([kk * kk, r * k2 * rkw_ref[...]], ones)
        kkn = kk * lax.rsqrt(jnp.maximum(ssq, 1e-24))
        hi, mid, lo = _split3(ld)
        cum3 = _dot(tri_ref[...], jnp.concatenate([hi, mid, lo], axis=1))
        cum = cum3[:, :BRANCH_W] + cum3[:, BRANCH_W:2 * BRANCH_W] + cum3[:, 2 * BRANCH_W:]
        e_incl = jnp.exp(cum)
        e_inv = jnp.exp(-cum)
        qk_s[rs, :] = (kkn * jnp.exp(cum - ld)).astype(BF16)
        qr_s[rs, :] = r * e_incl
        kb_s[rs, :] = (kkn * a * e_inv).astype(BF16)
        kk_s[rs, :] = (k2 * e_inv).astype(BF16)
        v_s[rs, :] = v.astype(BF16)
        ei_s[rs, :] = e_incl
        bonus_s[rs, :] = rk * v

    def chunk(c, g):
        r0 = c * CHUNK if isinstance(c, int) else pl.multiple_of(c * CHUNK, CHUNK)
        rs = pl.ds(r0, CHUNK)
        ls = slice(gw * g, gw * (g + 1))
        q_k = qk_s[rs, ls]
        q_r = qr_s[rs, ls]
        k_b = kb_s[rs, ls]
        k_k = kk_s[rs, ls]
        vb = v_s[rs, ls]
        q2 = jnp.concatenate([q_k, q_r.astype(BF16)], axis=0)
        bd_v = _block_rows(vb, head_masks)
        a_all = _dot_nt(q2, jnp.concatenate([_block_rows(k_b, head_masks), _block_rows(k_k, head_masks)], axis=0))
        yield
        a_kb = jnp.where(m_strict, a_all[:CHUNK, :gw], 0.0)
        a_rb = jnp.where(m_incl, a_all[CHUNK:, :gw], 0.0).astype(BF16)
        a_kk = jnp.where(m_strict, a_all[:CHUNK, gw:], 0.0).astype(BF16)
        a_rk = jnp.where(m_incl, a_all[CHUNK:, gw:], 0.0).astype(BF16)
        av = _dot(jnp.concatenate([a_kk, a_rk], axis=0), bd_v)
        npow = -a_kb
        tinv = eye_sbs + npow
        npow = _dot(npow.astype(BF16), _block_rows(npow, head_masks))
        yield
        akv = av[:CHUNK]
        o_v = av[CHUNK:]
        for it in range(1, log_seq):
            bd_n = _block_rows(npow, head_masks)
            if it + 1 < log_seq:
                both = _dot(jnp.concatenate([npow.astype(BF16), tinv.astype(BF16)], axis=0), bd_n)
                yield
                npow = both[:CHUNK]
                tinv = tinv + both[CHUNK:]
            else:
                last = _dot(tinv.astype(BF16), bd_n)
                yield
                tinv = tinv + last
        tw = _dot(tinv.astype(BF16),
                  jnp.concatenate([_block_rows(q_k, head_masks), _block_rows(-akv, head_masks)], axis=1))
        yield
        qt_k = tw[:, :gw]
        w_loc = tw[:, gw:]
        d1 = _dot(a_rb, jnp.concatenate([_block_rows(qt_k, head_masks), _block_rows(w_loc, head_masks)], axis=1))
        yield
        qh_r = q_r - d1[:, :gw]
        o_loc = d1[:, gw:] + o_v

        def update_state(sq, zz, pc):
            for hh in range(grp_heads):
                hr = slice(RWKV_HEAD * hh, RWKV_HEAD * (hh + 1))
                hl = slice(LANES * (hh // 2), LANES * (hh // 2 + 1))
                piece = (sbd[sq, g, hr, hl] + jnp.where(half_masks[hh % 2], zz[hr, hl], 0.0)) * pc[:, hl]
                sbd[sq, g, hr, hl] = piece
                sbb[sq, g, hr, hl] = piece.astype(BF16)

        if spc == 1:
            xs = _dot_nt(jnp.concatenate([qt_k.astype(BF16), qh_r.astype(BF16)], axis=0), sbb[c, g])
            yield
            u = w_loc - xs[:CHUNK]
            o_s[rs, ls] = xs[CHUNK:] + o_loc
            zz = _dot_tn(jnp.concatenate([u.astype(BF16), vb], axis=0), jnp.concatenate([k_b, k_k], axis=0))
            yield
            update_state(c, zz, ei_s[pl.ds(r0 + CHUNK - 1, 1), ls])
        else:
            kb_f = k_b.astype(F32)
            kk_f = k_k.astype(F32)
            v_f = vb.astype(F32)
            o_rows = []
            for q in range(spc):
                qs = slice(nt * q, nt * (q + 1))
                sq = c * spc + q
                xs = _dot_nt(jnp.concatenate([qt_k[qs], qh_r[qs]], axis=0).astype(BF16), sbb[sq, g])
                u = w_loc[qs] - xs[:nt]
                o_rows.append(xs[nt:] + o_loc[qs])
                zz = _dot_tn(jnp.concatenate([u, v_f[qs]], axis=0).astype(BF16),
                             jnp.concatenate([kb_f[qs], kk_f[qs]], axis=0).astype(BF16))
                update_state(sq, zz, ei_s[pl.ds(r0 + nt * (q + 1) - 1, 1), ls])
            o_s[rs, ls] = jnp.concatenate(o_rows, axis=0)

    def run_together(instances):
        live = [chunk(c, g) for c, g in instances]
        while live:
            still = []
            for gen in live:
                try:
                    next(gen)
                    still.append(gen)
                except StopIteration:
                    pass
            live = still

    if n_ch <= RW_UNROLL:
        run_together([(c, g) for c in range(n_ch) for g in range(n_grp)])
    else:
        def loop_body(cc, carry_):
            run_together([(cc * RW_UNROLL + i, g) for i in range(RW_UNROLL) for g in range(n_grp)])
            return carry_
        lax.fori_loop(0, n_ch // RW_UNROLL, loop_body, 0)

    ybs = []
    for r0 in range(0, rows, part):
        rs = slice(r0, r0 + part)
        o = o_s[rs, :]
        mean = _seg_sums([o], ones)[0] * (1.0 / RWKV_HEAD)
        dev = o - mean
        var = _seg_sums([dev * dev], ones)[0] * (1.0 / RWKV_HEAD)
        o = dev * lax.rsqrt(var + GN_EPS) * gnw_ref[...] + gnb_ref[...] + bonus_s[rs, :]
        ybs.append((o * _silu(z_s[rs, :])).astype(BF16))
    y_tm = _dot(permt_ref[...], jnp.concatenate(ybs, axis=0))
    y_ref[...] = y_tm.astype(y_ref.dtype).reshape(y_ref.shape)
    shout_ref[...] = carry[...]

    @pl.when(j == pl.num_programs(1) - 1)
    def _():
        for q in range(nbb):
            for g in range(n_grp):
                for hh in range(grp_heads):
                    hr = slice(RWKV_HEAD * hh, RWKV_HEAD * (hh + 1))
                    hl = slice(RWKV_HEAD * hh, RWKV_HEAD * (hh + 1))
                    sout_ref[q, gw * g + RWKV_HEAD * hh:gw * g + RWKV_HEAD * (hh + 1), :] = sbd[q, g, hr, hl]


def _rw_constants(nt, nbb, part):
    rows = nt * nbb
    r = np.arange(rows)
    src = (r % nt) * nbb + (r // nt)
    perm = np.zeros((rows, rows), np.float32)
    perm[r, src] = 1.0
    t = np.arange(part)
    tri = ((t[:, None] // nt) == (t[None, :] // nt)) & (t[None, :] <= t[:, None])
    return jnp.asarray(perm, BF16), jnp.asarray(perm.T, BF16), jnp.asarray(tri, BF16)


def _rw_call(x, shift0, s0, wl, *, seq, batch, nt, nbb, y_dtype):
    rows = nt * nbb
    part = min(rows, 256)
    gw = RW_GROUP if nt == CHUNK else MXU_TILE
    grid = (batch // nbb, seq // nt)
    if nbb == batch:
        x_spec = pl.BlockSpec((rows, D_MODEL), lambda i, j: (j, 0))
        y_spec = pl.BlockSpec((rows, BRANCH_W), lambda i, j: (j, 0))
        y_shape = jax.ShapeDtypeStruct((seq * batch, BRANCH_W), y_dtype)
    else:
        x_spec = pl.BlockSpec((nt, nbb, D_MODEL), lambda i, j: (j, i, 0))
        y_spec = pl.BlockSpec((nt, nbb, BRANCH_W), lambda i, j: (j, i, 0))
        y_shape = jax.ShapeDtypeStruct((seq, batch, BRANCH_W), y_dtype)
    st_spec = pl.BlockSpec((nbb, BRANCH_W, RWKV_HEAD), lambda i, j: (i, 0, 0))
    sh_spec = pl.BlockSpec((nbb, RWKV_SHIFT_W), lambda i, j: (i, 0))
    perm, permt, tri = _rw_constants(nt, nbb, part)
    weights = (wl["g_pre"], wl["w_rw"], perm, permt, wl["rw_mu"], wl["rw_w0"], wl["rw_lora"], wl["rw_a0"],
               wl["rw_kk"], wl["rw_ka"], wl["rw_rk"], wl["rw_gnw"], wl["rw_gnb"], wl["rw_ones"], tri)
    work = lambda dt: pltpu.VMEM((rows, BRANCH_W), dt)
    return pl.pallas_call(
        functools.partial(_rw_body, nt=nt, nbb=nbb, part=part, gw=gw),
        grid=grid,
        in_specs=[x_spec, st_spec, sh_spec] + [_full_spec(w.shape) for w in weights],
        out_specs=[y_spec, st_spec, sh_spec],
        out_shape=[y_shape, jax.ShapeDtypeStruct((batch, BRANCH_W, RWKV_HEAD), F32),
                   jax.ShapeDtypeStruct((batch, RWKV_SHIFT_W), F32)],
        scratch_shapes=[pltpu.VMEM((nbb, BRANCH_W // gw, gw, gw), F32),
                        pltpu.VMEM((nbb, BRANCH_W // gw, gw, gw), BF16),
                        pltpu.VMEM((nbb, RWKV_SHIFT_W), F32),
                        work(BF16), work(F32), work(BF16), work(BF16), work(BF16), work(F32), work(F32),
                        work(F32), work(F32)],
        compiler_params=_params(("arbitrary", "arbitrary")),
    )(x, s0, shift0, *weights)


def _merge_body(x_ref, ys5_ref, yrw_ref, ypl_ref, gpre_ref, wg_ref, wu_ref, wo_ref, gpost_ref, o_ref):
    x = x_ref[...]
    h = _rms_scale(x, gpre_ref[...]).astype(BF16)
    merged = None
    for i, y_ref in enumerate((ys5_ref, yrw_ref, ypl_ref)):
        gate = _sigmoid(_dot(h, wg_ref[:, D_MODEL * i:D_MODEL * (i + 1)]))
        term = gate * _dot(y_ref[...].astype(BF16), wu_ref[i])
        merged = term if merged is None else merged + term
    out = _dot(merged.astype(BF16), wo_ref[...])
    o_ref[...] = x + _rms_scale(out, gpost_ref[...])


def _merge_call(x, ys5, yrw, ypl, wl, *, tm):
    m = x.shape[0]
    weights = (wl["g_pre"], wl["w_gate"], wl["w_up"], wl["w_o"], wl["g_post"])
    row_spec = lambda w: pl.BlockSpec((tm, w), lambda i: (i, 0))
    return pl.pallas_call(
        _merge_body,
        grid=(m // tm,),
        in_specs=[row_spec(D_MODEL), row_spec(BRANCH_W), row_spec(BRANCH_W), row_spec(BRANCH_W)]
        + [_full_spec(w.shape) for w in weights],
        out_specs=row_spec(D_MODEL),
        out_shape=jax.ShapeDtypeStruct((m, D_MODEL), F32),
        compiler_params=_params(("parallel",)),
    )(x, ys5, yrw, ypl, *weights)


def _block_diag(blocks):
    n, r, c = blocks.shape
    eye = jnp.eye(n, dtype=blocks.dtype)
    return jnp.einsum("grc,gh->grhc", blocks, eye).reshape(n * r, n * c)


def _prep_layer(l, norm_pre, norm_post, w_in, s5_lam_re, s5_lam_im, s5_log_dt, s5_b_re, s5_b_im,
                s5_c_re, s5_c_im, s5_d, s5_w_glu, rwkv_mu, rwkv_w0, rwkv_w_w2, rwkv_a0, rwkv_w_a2,
                rwkv_k_k, rwkv_k_a, rwkv_r_k, rwkv_gn_w, rwkv_gn_b, pool_w, pool_scale,
                w_up_s5, w_up_rwkv, w_up_pool, w_o):
    row = lambda a: a.astype(F32).reshape(1, -1)
    wl = {}
    wl["g_pre"] = row(norm_pre[l])
    wl["g_post"] = row(norm_post[l])
    win = w_in[l].astype(BF16)
    wl["w_s5"] = win[:, 0:1024]
    wl["w_rw"] = win[:, 1024:3200]
    wl["w_pool"] = win[:, 3200:4224]
    wl["w_gate"] = win[:, 4224:7296]

    lr = s5_lam_re[l].astype(F32)
    li = s5_lam_im[l].astype(F32)
    dt = jnp.exp(s5_log_dt[l].astype(F32))[:, None]
    mag = jnp.exp(lr * dt)
    ab_r = mag * jnp.cos(li * dt)
    ab_i = mag * jnp.sin(li * dt)
    den = lr * lr + li * li
    nr = ab_r - 1.0
    co_r = ((nr * lr + ab_i * li) / den)[:, :, None]
    co_i = ((ab_i * lr - nr * li) / den)[:, :, None]
    b_re = s5_b_re[l].astype(F32)
    b_im = s5_b_im[l].astype(F32)
    bf_re = co_r * b_re - co_i * b_im
    bf_im = co_r * b_im + co_i * b_re
    gh = S5_GROUPS // 2
    wb = []
    for k in range(2):
        gs = slice(gh * k, gh * (k + 1))
        wb.append(jnp.concatenate([_block_diag(jnp.swapaxes(bf_re[gs], 1, 2)),
                                   _block_diag(jnp.swapaxes(bf_im[gs], 1, 2))], axis=1))
    wl["s5_wb"] = jnp.stack(wb).astype(BF16)
    wl["s5_ar"] = ab_r.reshape(1, S5_NSTATE)
    wl["s5_ai"] = ab_i.reshape(1, S5_NSTATE)
    c_re = s5_c_re[l].astype(F32)
    c_im = s5_c_im[l].astype(F32)
    cc = []
    for k in range(2):
        gs = slice(gh * k, gh * (k + 1))
        cc.append(jnp.concatenate([_block_diag(jnp.swapaxes(c_re[gs], 1, 2)),
                                   _block_diag(jnp.swapaxes(-c_im[gs], 1, 2))], axis=0))
    wl["s5_cc"] = jnp.stack(cc).astype(BF16)
    wl["s5_d"] = row(s5_d[l])
    wl["s5_wglu"] = s5_w_glu[l].astype(BF16)

    wl["rw_mu"] = row(rwkv_mu[l])
    wl["rw_w0"] = row(rwkv_w0[l])
    zeros = jnp.zeros((LORA_RANK, BRANCH_W), F32)
    wl["rw_lora"] = jnp.concatenate(
        [jnp.concatenate([rwkv_w_w2[l].astype(F32), zeros], axis=1),
         jnp.concatenate([zeros, rwkv_w_a2[l].astype(F32)], axis=1)], axis=0).astype(BF16)
    wl["rw_a0"] = row(rwkv_a0[l])
    wl["rw_kk"] = row(rwkv_k_k[l])
    wl["rw_ka"] = row(rwkv_k_a[l])
    wl["rw_rk"] = row(rwkv_r_k[l])
    wl["rw_gnw"] = row(rwkv_gn_w[l])
    wl["rw_gnb"] = row(rwkv_gn_b[l])
    wl["rw_ones"] = _block_diag(jnp.ones((MXU_TILE // RWKV_HEAD, RWKV_HEAD, RWKV_HEAD), F32)).astype(BF16)

    wl["pool_w"] = _block_diag(pool_w[l].astype(F32)).astype(BF16)
    wl["pool_scale"] = row(pool_scale[l])
    wl["w_up"] = jnp.stack([w_up_s5[l], w_up_rwkv[l], w_up_pool[l]]).astype(BF16)
    wl["w_o"] = w_o[l].astype(BF16)
    return wl


def _layer(x, states, wl, *, seq, batch, start, prompt):
    s5r, s5i, rws, rwsh, pbuf = states
    if prompt:
        nt, nb, nbb, y_dtype = 64, batch, batch, BF16
    else:
        nt, nb, nbb, y_dtype = seq, min(batch, 32), min(batch, 16), F32
    x3 = x.reshape(seq, batch, D_MODEL)
    flat = lambda y: y.reshape(seq * batch, BRANCH_W)
    if nb == batch:
        x_tm = x
        pbuf_in = pbuf.reshape(POOL_BUF * batch, BRANCH_W)
    else:
        x_tm = x3
        pbuf_in = pbuf
    ys5, n_re, n_im = _s5_call(x_tm, wl, s5r, s5i, seq=seq, batch=batch, nt=nt, nb=nb, y_dtype=y_dtype)
    ypl, n_buf = _pool_call(x_tm, wl, pbuf_in, seq=seq, batch=batch, nt=nt, nb=nb, start=start, y_dtype=y_dtype)
    yrw, n_s, n_shift = _rw_call(x if nbb == batch else x3, rwsh, rws, wl, seq=seq, batch=batch, nt=nt, nbb=nbb,
                                 y_dtype=y_dtype)
    x_new = _merge_call(x, flat(ys5), flat(yrw), flat(ypl), wl, tm=512)
    n_buf = n_buf.reshape(POOL_BUF, batch, BRANCH_W)
    return x_new, (n_re, n_im, n_s, n_shift, n_buf)


def kernel(x_prompt, x_sample, state_s5_re, state_s5_im, state_rwkv, state_shift, state_pool, norm_pre, norm_post, w_in, s5_lam_re, s5_lam_im, s5_log_dt, s5_b_re, s5_b_im, s5_c_re, s5_c_im, s5_d, s5_w_glu, rwkv_mu, rwkv_w0, rwkv_w_w2, rwkv_a0, rwkv_w_a2, rwkv_k_k, rwkv_k_a, rwkv_r_k, rwkv_gn_w, rwkv_gn_b, pool_w, pool_scale, w_up_s5, w_up_rwkv, w_up_pool, w_o):
    depth = w_in.shape[0]
    bp, tp, _ = x_prompt.shape
    bs, ts, _ = x_sample.shape
    xp = jnp.transpose(x_prompt, (1, 0, 2)).reshape(tp * bp, D_MODEL)
    xs = jnp.transpose(x_sample, (1, 0, 2)).reshape(ts * bs, D_MODEL)
    outs_p = [[] for _ in range(5)]
    outs_s = [[] for _ in range(5)]
    for l in range(depth):
        wl = _prep_layer(l, norm_pre, norm_post, w_in, s5_lam_re, s5_lam_im, s5_log_dt, s5_b_re, s5_b_im,
                         s5_c_re, s5_c_im, s5_d, s5_w_glu, rwkv_mu, rwkv_w0, rwkv_w_w2, rwkv_a0, rwkv_w_a2,
                         rwkv_k_k, rwkv_k_a, rwkv_r_k, rwkv_gn_w, rwkv_gn_b, pool_w, pool_scale,
                         w_up_s5, w_up_rwkv, w_up_pool, w_o)
        zeros_p = (jnp.zeros((bp, S5_NSTATE), F32), jnp.zeros((bp, S5_NSTATE), F32),
                   jnp.zeros((bp, BRANCH_W, RWKV_HEAD), F32), jnp.zeros((bp, RWKV_SHIFT_W), F32),
                   jnp.zeros((POOL_BUF, bp, BRANCH_W), F32))
        xp, new_p = _layer(xp, zeros_p, wl, seq=tp, batch=bp, start=0, prompt=True)
        st_s = (state_s5_re[l].reshape(bs, S5_NSTATE), state_s5_im[l].reshape(bs, S5_NSTATE),
                state_rwkv[l].reshape(bs, BRANCH_W, RWKV_HEAD), state_shift[l],
                jnp.transpose(state_pool[l], (1, 0, 2)))
        xs, new_s = _layer(xs, st_s, wl, seq=ts, batch=bs, start=PAST_LEN, prompt=False)
        for lst, val in zip(outs_p, new_p):
            lst.append(val)
        for lst, val in zip(outs_s, new_s):
            lst.append(val)

    def finish(outs, b):
        re, im, rw, sh, pb = [jnp.stack(v) for v in outs]
        return (re.reshape(depth, b, S5_GROUPS, S5_STATE), im.reshape(depth, b, S5_GROUPS, S5_STATE),
                rw.reshape(depth, b, RWKV_HEADS, RWKV_HEAD, RWKV_HEAD), sh,
                jnp.transpose(pb, (0, 2, 1, 3)))

    y_p = jnp.transpose(xp.reshape(tp, bp, D_MODEL), (1, 0, 2))
    y_s = jnp.transpose(xs.reshape(ts, bs, D_MODEL), (1, 0, 2))
    return (y_p, y_s) + finish(outs_p, bp) + finish(outs_s, bs)
```

```python
import functools
import math

import jax
import jax.numpy as jnp
import numpy as np
from jax import lax
from jax.experimental import pallas as pl
from jax.experimental.pallas import tpu as pltpu

F32 = jnp.float32
BF16 = jnp.bfloat16

D_MODEL = 1024
BRANCH_W = 512
S5_GROUPS = 32
S5_GROUP = 16
S5_STATE = 64
S5_NSTATE = S5_GROUPS * S5_STATE
RWKV_HEADS = 8
RWKV_HEAD = 64
LORA_RANK = 64
RWKV_SHIFT_W = 3 * BRANCH_W + 2 * LORA_RANK
POOL_WINDOWS = (2, 4, 8, 16)
POOL_GW = 128
POOL_BUF = 15
NORM_EPS = 1e-6
GN_EPS = 64e-5
PAST_LEN = 16384

MXU_TILE = 256
LANES = 128
CHUNK = 64
RW_GROUP = 256
RW_UNROLL = 8
VMEM_LIMIT = 56 * 1024 * 1024


def _dot(a, b):
    return jnp.dot(a, b, preferred_element_type=F32)


def _dot_nt(a, b):
    return lax.dot_general(a, b, (((1,), (1,)), ((), ())), preferred_element_type=F32)


def _dot_tn(a, b):
    return lax.dot_general(a, b, (((0,), (0,)), ((), ())), preferred_element_type=F32)


def _split3(x):
    hi = x.astype(BF16)
    r1 = x - hi.astype(F32)
    mid = r1.astype(BF16)
    lo = (r1 - mid.astype(F32)).astype(BF16)
    return hi, mid, lo


def _rms_scale(x, g):
    ms = jnp.mean(x * x, axis=-1, keepdims=True)
    return x * lax.rsqrt(ms + NORM_EPS) * g


def _sigmoid(x):
    return jax.nn.sigmoid(x)


def _silu(x):
    return x * jax.nn.sigmoid(x)


def _gelu_tanh(x):
    c = math.sqrt(2.0 / math.pi)
    return 0.5 * x * (1.0 + jnp.tanh(c * (x + 0.044715 * (x * x * x))))


def _full_spec(shape):
    nd = len(shape)
    return pl.BlockSpec(shape, lambda *_: (0,) * nd)


def _layer_spec(shape, l):
    nd = len(shape) - 1
    return pl.BlockSpec((None,) + tuple(shape[1:]), lambda *_: (l,) + (0,) * nd)


def _params(sem):
    return pltpu.CompilerParams(dimension_semantics=sem, vmem_limit_bytes=VMEM_LIMIT)


def _s5_body(x_ref, g_ref, w_ref, wb_ref, ar_ref, ai_ref, cc_ref, d_ref, wglu_ref, h0r_ref, h0i_ref,
             y_ref, nr_ref, ni_ref, bbr, bbi, cr, ci, *, nt, nb):
    j = pl.program_id(1)
    rows = nt * nb
    half = S5_NSTATE // 2

    @pl.when(j == 0)
    def _():
        cr[...] = h0r_ref[...]
        ci[...] = h0i_ref[...]

    x = x_ref[...].reshape(rows, D_MODEL)
    h = _rms_scale(x, g_ref[...]).astype(BF16)
    sxz = _dot(h, w_ref[...])
    sx = sxz[:, :BRANCH_W]
    sz = sxz[:, BRANCH_W:]
    ub = sx.astype(BF16)
    for k in range(2):
        o = _dot(ub[:, MXU_TILE * k:MXU_TILE * (k + 1)], wb_ref[k])
        bbr[:, half * k:half * (k + 1)] = o[:, :half]
        bbi[:, half * k:half * (k + 1)] = o[:, half:]

    cw = max(128, min(1024, (8 * 1024) // nb))
    for c0 in range(0, S5_NSTATE, cw):
        cs = slice(c0, c0 + cw)
        a_r = jnp.broadcast_to(ar_ref[:, cs], (nb, cw))
        a_i = jnp.broadcast_to(ai_ref[:, cs], (nb, cw))

        def step(t, carry, cs=cs, a_r=a_r, a_i=a_i):
            hr, hi = carry
            rs = pl.ds(pl.multiple_of(t * nb, nb), nb)
            nhr = a_r * hr - a_i * hi + bbr[rs, cs]
            nhi = a_r * hi + a_i * hr + bbi[rs, cs]
            bbr[rs, cs] = nhr
            bbi[rs, cs] = nhi
            return nhr, nhi

        hr, hi = lax.fori_loop(0, nt, step, (cr[:, cs], ci[:, cs]), unroll=min(nt, 8))
        cr[:, cs] = hr
        ci[:, cs] = hi

    ys = []
    for k in range(2):
        ks = slice(half * k, half * (k + 1))
        lhs = jnp.concatenate([bbr[:, ks].astype(BF16), bbi[:, ks].astype(BF16)], axis=1)
        ys.append(_dot(lhs, cc_ref[k]))
    y = jnp.concatenate(ys, axis=1) + d_ref[...] * sx
    y = _gelu_tanh(y)
    y = y * _sigmoid(_dot(y.astype(BF16), wglu_ref[...]))
    y_ref[...] = (y * _silu(sz)).astype(y_ref.dtype).reshape(y_ref.shape)
    nr_ref[...] = cr[...]
    ni_ref[...] = ci[...]


def _s5_call(x, wl, h0r, h0i, *, l, sl, seq, batch, nt, nb, y_dtype):
    rows = nt * nb
    grid = (batch // nb, seq // nt)
    if nb == batch:
        x_spec = pl.BlockSpec((rows, D_MODEL), lambda i, j: (j, 0))
        y_spec = pl.BlockSpec((rows, BRANCH_W), lambda i, j: (j, 0))
        y_shape = jax.ShapeDtypeStruct((seq * batch, BRANCH_W), y_dtype)
    else:
        x_spec = pl.BlockSpec((nt, nb, D_MODEL), lambda i, j: (j, i, 0))
        y_spec = pl.BlockSpec((nt, nb, BRANCH_W), lambda i, j: (j, i, 0))
        y_shape = jax.ShapeDtypeStruct((seq, batch, BRANCH_W), y_dtype)
    st_in = pl.BlockSpec((None, nb, S5_NSTATE), lambda i, j: (sl, i, 0))
    st_spec = pl.BlockSpec((nb, S5_NSTATE), lambda i, j: (i, 0))
    st_shape = jax.ShapeDtypeStruct((batch, S5_NSTATE), F32)
    weights = (wl["g_pre"], wl["w_s5"], wl["s5_wb"], wl["s5_ar"], wl["s5_ai"], wl["s5_cc"], wl["s5_d"],
               wl["s5_wglu"])
    return pl.pallas_call(
        functools.partial(_s5_body, nt=nt, nb=nb),
        grid=grid,
        in_specs=[x_spec] + [_layer_spec(w.shape, l) for w in weights] + [st_in, st_in],
        out_specs=[y_spec, st_spec, st_spec],
        out_shape=[y_shape, st_shape, st_shape],
        scratch_shapes=[pltpu.VMEM((rows, S5_NSTATE), F32), pltpu.VMEM((rows, S5_NSTATE), F32),
                        pltpu.VMEM((nb, S5_NSTATE), F32), pltpu.VMEM((nb, S5_NSTATE), F32)],
        compiler_params=_params(("arbitrary", "arbitrary")),
    )(x, *weights, h0r, h0i)


def _pool_body(x_ref, g_ref, w_ref, pw_ref, ps_ref, buf_ref, y_ref, nbuf_ref, ext, *, nt, nb, start):
    j = pl.program_id(1)
    rows = nt * nb
    prev = POOL_BUF * nb

    @pl.when(j == 0)
    def _():
        ext[0:prev, :] = buf_ref[...].reshape(prev, BRANCH_W)

    x = x_ref[...].reshape(rows, D_MODEL)
    h = _rms_scale(x, g_ref[...]).astype(BF16)
    pxz = _dot(h, w_ref[...])
    px = pxz[:, :BRANCH_W]
    pz = pxz[:, BRANCH_W:]
    ext[prev:prev + rows, :] = px

    t_local = lax.broadcasted_iota(jnp.int32, (rows, POOL_GW), 0) // nb
    pos1 = start + j * nt + t_local + 1
    outs = []
    for gi, win in enumerate(POOL_WINDOWS):
        cs = slice(POOL_GW * gi, POOL_GW * (gi + 1))
        acc = ext[prev:prev + rows, cs]
        for q in range(1, win):
            acc = acc + ext[prev - q * nb:prev - q * nb + rows, cs]
        cnt = jnp.minimum(pos1, win).astype(F32)
        outs.append(acc / cnt)
    pooled = jnp.concatenate(outs, axis=1) - px
    mixed = _dot(pooled.astype(BF16), pw_ref[...]) * ps_ref[...]
    y_ref[...] = (mixed * _silu(pz)).astype(y_ref.dtype).reshape(y_ref.shape)
    last = ext[rows:rows + prev, :]
    nbuf_ref[...] = last.reshape(nbuf_ref.shape)
    ext[0:prev, :] = last


def _pool_call(x, wl, buf, *, l, sl, seq, batch, nt, nb, start, y_dtype):
    rows = nt * nb
    grid = (batch // nb, seq // nt)
    if nb == batch:
        x_spec = pl.BlockSpec((rows, D_MODEL), lambda i, j: (j, 0))
        y_spec = pl.BlockSpec((rows, BRANCH_W), lambda i, j: (j, 0))
        y_shape = jax.ShapeDtypeStruct((seq * batch, BRANCH_W), y_dtype)
        b_in = pl.BlockSpec((None, POOL_BUF * nb, BRANCH_W), lambda i, j: (sl, 0, 0))
        b_spec = pl.BlockSpec((POOL_BUF * nb, BRANCH_W), lambda i, j: (0, 0))
        b_shape = jax.ShapeDtypeStruct((POOL_BUF * batch, BRANCH_W), F32)
    else:
        x_spec = pl.BlockSpec((nt, nb, D_MODEL), lambda i, j: (j, i, 0))
        y_spec = pl.BlockSpec((nt, nb, BRANCH_W), lambda i, j: (j, i, 0))
        y_shape = jax.ShapeDtypeStruct((seq, batch, BRANCH_W), y_dtype)
        b_in = pl.BlockSpec((None, POOL_BUF, nb, BRANCH_W), lambda i, j: (sl, 0, i, 0))
        b_spec = pl.BlockSpec((POOL_BUF, nb, BRANCH_W), lambda i, j: (0, i, 0))
        b_shape = jax.ShapeDtypeStruct((POOL_BUF, batch, BRANCH_W), F32)
    weights = (wl["g_pre"], wl["w_pool"], wl["pool_w"], wl["pool_scale"])
    return pl.pallas_call(
        functools.partial(_pool_body, nt=nt, nb=nb, start=start),
        grid=grid,
        in_specs=[x_spec] + [_layer_spec(w.shape, l) for w in weights] + [b_in],
        out_specs=[y_spec, b_spec],
        out_shape=[y_shape, b_shape],
        scratch_shapes=[pltpu.VMEM(((POOL_BUF + nt) * nb, BRANCH_W), F32)],
        compiler_params=_params(("arbitrary", "arbitrary")),
    )(x, *weights, buf)


def _seg_sums(xs, ones):
    rows = xs[0].shape[0]
    stacked = jnp.concatenate([x.astype(BF16) for x in xs], axis=0)
    s = jnp.concatenate([_dot(stacked[:, MXU_TILE * g:MXU_TILE * (g + 1)], ones)
                         for g in range(BRANCH_W // MXU_TILE)], axis=1)
    return [s[i * rows:(i + 1) * rows] for i in range(len(xs))]


def _block_rows(y, head_masks):
    yb = y.astype(BF16)
    return jnp.concatenate([yb * m for m in head_masks], axis=0)


def _rw_body(x_ref, s0_ref, sh_ref, g_ref, w_ref, perm_ref, permt_ref, mu_ref, w0_ref, wl_ref, a0_ref,
             kkw_ref, kaw_ref, rkw_ref, gnw_ref, gnb_ref, ones_ref, tri_ref,
             y_ref, sout_ref, shout_ref,
             sbd, sbb, carry, qk_s, qr_s, kb_s, kk_s, v_s, ei_s, bonus_s, z_s, o_s,
             *, nt, nbb, part, gw):
    j = pl.program_id(1)
    rows = nt * nbb
    n_ch = rows // CHUNK
    spc = CHUNK // nt
    log_seq = nt.bit_length() - 1

    n_grp = BRANCH_W // gw
    grp_heads = gw // RWKV_HEAD

    lane = lax.broadcasted_iota(jnp.int32, (CHUNK, gw), 1)
    row64 = lax.broadcasted_iota(jnp.int32, (CHUNK, gw), 0)
    src = lane & (CHUNK - 1)
    same_seq = (row64 >> log_seq) == (src >> log_seq)
    m_strict = same_seq & (src < row64)
    m_incl = same_seq & (src <= row64)
    eye_sbs = (src == row64).astype(F32)
    head_masks = [((lane >> 6) == hh).astype(BF16) for hh in range(grp_heads)]
    half_lane = lax.broadcasted_iota(jnp.int32, (RWKV_HEAD, LANES), 1)
    half_masks = [half_lane < RWKV_HEAD, half_lane >= RWKV_HEAD]

    @pl.when(j == 0)
    def _():
        zero_blk = jnp.zeros((RWKV_HEAD, RWKV_HEAD), F32)
        for q in range(nbb):
            for g in range(n_grp):
                sbd[q, g] = jnp.zeros((gw, gw), F32)
                sbb[q, g] = jnp.zeros((gw, gw), BF16)
                for hh in range(grp_heads):
                    hr = slice(RWKV_HEAD * hh, RWKV_HEAD * (hh + 1))
                    hl = slice(LANES * (hh // 2), LANES * (hh // 2 + 1))
                    blk = s0_ref[q, gw * g + RWKV_HEAD * hh:gw * g + RWKV_HEAD * (hh + 1), :]
                    piece = jnp.concatenate([blk, zero_blk] if hh % 2 == 0 else [zero_blk, blk], axis=1)
                    sbd[q, g, hr, hl] = piece
                    sbb[q, g, hr, hl] = piece.astype(BF16)
        carry[...] = sh_ref[...]

    x = x_ref[...].reshape(rows, D_MODEL)
    h = _rms_scale(x, g_ref[...]).astype(BF16)
    hb = _dot(perm_ref[...], h).astype(BF16)
    ones = ones_ref[...]

    is_first = (lax.broadcasted_iota(jnp.int32, (part, RWKV_SHIFT_W), 0) & (nt - 1)) == 0
    lane128 = lax.broadcasted_iota(jnp.int32, (part, 2 * LORA_RANK), 1)
    for r0 in range(0, rows, part):
        rs = slice(r0, r0 + part)
        q0 = r0 // nt
        pz = _dot(hb[rs], w_ref[...])
        p = pz[:, :RWKV_SHIFT_W]
        z_s[rs, :] = pz[:, RWKV_SHIFT_W:]
        first = jnp.concatenate(
            [jnp.broadcast_to(carry[q0 + q:q0 + q + 1, :], (nt, RWKV_SHIFT_W)) for q in range(part // nt)], axis=0)
        prev = jnp.where(is_first, first, pltpu.roll(p, 1, 0))
        for q in range(part // nt):
            carry[q0 + q:q0 + q + 1, :] = p[nt * (q + 1) - 1:nt * (q + 1), :]
        pm = p + (prev - p) * mu_ref[...]
        r = pm[:, 0:BRANCH_W]
        k = pm[:, BRANCH_W:2 * BRANCH_W]
        v = pm[:, 2 * BRANCH_W:3 * BRANCH_W]
        ll = pm[:, 3 * BRANCH_W:]
        xl = jnp.where(lane128 < LORA_RANK, jnp.tanh(ll), ll).astype(BF16)
        dl = _dot(xl, wl_ref[...])
        w_log = -jax.nn.softplus(-(w0_ref[...] + dl[:, :BRANCH_W])) - 0.5
        ld = -jnp.exp(w_log)
        a = _sigmoid(a0_ref[...] + dl[:, BRANCH_W:])
        kk = k * kkw_ref[...]
        k2 = k * (1.0 + (a - 1.0) * kaw_ref[...])
        ssq, rk = _seg_sums([kk * kk, r * k2 * rkw_ref[...]], ones)
        kkn = kk * lax.rsqrt(jnp.maximum(ssq, 1e-24))
        hi, mid, lo = _split3(ld)
        cum3 = _dot(tri_ref[...], jnp.concatenate([hi, mid, lo], axis=1))
        cum = cum3[:, :BRANCH_W] + cum3[:, BRANCH_W:2 * BRANCH_W] + cum3[:, 2 * BRANCH_W:]
        e_incl = jnp.exp(cum)
        e_inv = jnp.exp(-cum)
        qk_s[rs, :] = (kkn * jnp.exp(cum - ld)).astype(BF16)
        qr_s[rs, :] = r * e_incl
        kb_s[rs, :] = (kkn * a * e_inv).astype(BF16)
        kk_s[rs, :] = (k2 * e_inv).astype(BF16)
        v_s[rs, :] = v.astype(BF16)
        ei_s[rs, :] = e_incl
        bonus_s[rs, :] = rk * v

    def chunk(c, g):
        r0 = c * CHUNK if isinstance(c, int) else pl.multiple_of(c * CHUNK, CHUNK)
        rs = pl.ds(r0, CHUNK)
        ls = slice(gw * g, gw * (g + 1))
        q_k = qk_s[rs, ls]
        q_r = qr_s[rs, ls]
        k_b = kb_s[rs, ls]
        k_k = kk_s[rs, ls]
        vb = v_s[rs, ls]
        q2 = jnp.concatenate([q_k, q_r.astype(BF16)], axis=0)
        bd_v = _block_rows(vb, head_masks)
        a_all = _dot_nt(q2, jnp.concatenate([_block_rows(k_b, head_masks), _block_rows(k_k, head_masks)], axis=0))
        yield
        a_kb = jnp.where(m_strict, a_all[:CHUNK, :gw], 0.0)
        a_rb = jnp.where(m_incl, a_all[CHUNK:, :gw], 0.0).astype(BF16)
        a_kk = jnp.where(m_strict, a_all[:CHUNK, gw:], 0.0).astype(BF16)
        a_rk = jnp.where(m_incl, a_all[CHUNK:, gw:], 0.0).astype(BF16)
        av = _dot(jnp.concatenate([a_kk, a_rk], axis=0), bd_v)
        npow = -a_kb
        tinv = eye_sbs + npow
        npow = _dot(npow.astype(BF16), _block_rows(npow, head_masks))
        yield
        akv = av[:CHUNK]
        o_v = av[CHUNK:]
        for it in range(1, log_seq):
            bd_n = _block_rows(npow, head_masks)
            if it + 1 < log_seq:
                both = _dot(jnp.concatenate([npow.astype(BF16), tinv.astype(BF16)], axis=0), bd_n)
                yield
                npow = both[:CHUNK]
                tinv = tinv + both[CHUNK:]
            else:
                last = _dot(tinv.astype(BF16), bd_n)
                yield
                tinv = tinv + last
        tw = _dot(tinv.astype(BF16),
                  jnp.concatenate([_block_rows(q_k, head_masks), _block_rows(-akv, head_masks)], axis=1))
        yield
        qt_k = tw[:, :gw]
        w_loc = tw[:, gw:]
        d1 = _dot(a_rb, jnp.concatenate([_block_rows(qt_k, head_masks), _block_rows(w_loc, head_masks)], axis=1))
        yield
        qh_r = q_r - d1[:, :gw]
        o_loc = d1[:, gw:] + o_v

        def update_state(sq, zz, pc):
            for hh in range(grp_heads):
                hr = slice(RWKV_HEAD * hh, RWKV_HEAD * (hh + 1))
                hl = slice(LANES * (hh // 2), LANES * (hh // 2 + 1))
                piece = (sbd[sq, g, hr, hl] + jnp.where(half_masks[hh % 2], zz[hr, hl], 0.0)) * pc[:, hl]
                sbd[sq, g, hr, hl] = piece
                sbb[sq, g, hr, hl] = piece.astype(BF16)

        if spc == 1:
            xs = _dot_nt(jnp.concatenate([qt_k.astype(BF16), qh_r.astype(BF16)], axis=0), sbb[c, g])
            yield
            u = w_loc - xs[:CHUNK]
            o_s[rs, ls] = xs[CHUNK:] + o_loc
            zz = _dot_tn(jnp.concatenate([u.astype(BF16), vb], axis=0), jnp.concatenate([k_b, k_k], axis=0))
            yield
            update_state(c, zz, ei_s[pl.ds(r0 + CHUNK - 1, 1), ls])
        else:
            kb_f = k_b.astype(F32)
            kk_f = k_k.astype(F32)
            v_f = vb.astype(F32)
            seqs = [slice(nt * q, nt * (q + 1)) for q in range(spc)]
            xs = [_dot_nt(jnp.concatenate([qt_k[qs], qh_r[qs]], axis=0).astype(BF16), sbb[c * spc + q, g])
                  for q, qs in enumerate(seqs)]
            yield
            o_s[rs, ls] = jnp.concatenate([x[nt:] for x in xs], axis=0) + o_loc
            zz = [_dot_tn(jnp.concatenate([w_loc[qs] - x[:nt], v_f[qs]], axis=0).astype(BF16),
                          jnp.concatenate([kb_f[qs], kk_f[qs]], axis=0).astype(BF16))
                  for x, qs in zip(xs, seqs)]
            yield
            for q in range(spc):
                update_state(c * spc + q, zz[q], ei_s[pl.ds(r0 + nt * (q + 1) - 1, 1), ls])

    def run_together(instances):
        live = [chunk(c, g) for c, g in instances]
        while live:
            still = []
            for gen in live:
                try:
                    next(gen)
                    still.append(gen)
                except StopIteration:
                    pass
            live = still

    if n_ch <= RW_UNROLL:
        run_together([(c, g) for c in range(n_ch) for g in range(n_grp)])
    else:
        def loop_body(cc, carry_):
            run_together([(cc * RW_UNROLL + i, g) for i in range(RW_UNROLL) for g in range(n_grp)])
            return carry_
        lax.fori_loop(0, n_ch // RW_UNROLL, loop_body, 0)

    ybs = []
    for r0 in range(0, rows, part):
        rs = slice(r0, r0 + part)
        o = o_s[rs, :]
        mean = _seg_sums([o], ones)[0] * (1.0 / RWKV_HEAD)
        dev = o - mean
        var = _seg_sums([dev * dev], ones)[0] * (1.0 / RWKV_HEAD)
        o = dev * lax.rsqrt(var + GN_EPS) * gnw_ref[...] + gnb_ref[...] + bonus_s[rs, :]
        ybs.append((o * _silu(z_s[rs, :])).astype(BF16))
    y_tm = _dot(permt_ref[...], jnp.concatenate(ybs, axis=0))
    y_ref[...] = y_tm.astype(y_ref.dtype).reshape(y_ref.shape)
    shout_ref[...] = carry[...]

    @pl.when(j == pl.num_programs(1) - 1)
    def _():
        for q in range(nbb):
            for g in range(n_grp):
                for hh in range(grp_heads):
                    hr = slice(RWKV_HEAD * hh, RWKV_HEAD * (hh + 1))
                    hl = slice(RWKV_HEAD * hh, RWKV_HEAD * (hh + 1))
                    sout_ref[q, gw * g + RWKV_HEAD * hh:gw * g + RWKV_HEAD * (hh + 1), :] = sbd[q, g, hr, hl]


def _rw_constants(nt, nbb, part):
    rows = nt * nbb
    r = np.arange(rows)
    src = (r % nt) * nbb + (r // nt)
    perm = np.zeros((rows, rows), np.float32)
    perm[r, src] = 1.0
    t = np.arange(part)
    tri = ((t[:, None] // nt) == (t[None, :] // nt)) & (t[None, :] <= t[:, None])
    return jnp.asarray(perm, BF16), jnp.asarray(perm.T, BF16), jnp.asarray(tri, BF16)


def _rw_call(x, shift0, s0, wl, *, l, sl, seq, batch, nt, nbb, y_dtype):
    rows = nt * nbb
    part = min(rows, 256)
    gw = RW_GROUP if nt == CHUNK else MXU_TILE
    grid = (batch // nbb, seq // nt)
    if nbb == batch:
        x_spec = pl.BlockSpec((rows, D_MODEL), lambda i, j: (j, 0))
        y_spec = pl.BlockSpec((rows, BRANCH_W), lambda i, j: (j, 0))
        y_shape = jax.ShapeDtypeStruct((seq * batch, BRANCH_W), y_dtype)
    else:
        x_spec = pl.BlockSpec((nt, nbb, D_MODEL), lambda i, j: (j, i, 0))
        y_spec = pl.BlockSpec((nt, nbb, BRANCH_W), lambda i, j: (j, i, 0))
        y_shape = jax.ShapeDtypeStruct((seq, batch, BRANCH_W), y_dtype)
    st_in = pl.BlockSpec((None, nbb, BRANCH_W, RWKV_HEAD), lambda i, j: (sl, i, 0, 0))
    sh_in = pl.BlockSpec((None, nbb, RWKV_SHIFT_W), lambda i, j: (sl, i, 0))
    st_spec = pl.BlockSpec((nbb, BRANCH_W, RWKV_HEAD), lambda i, j: (i, 0, 0))
    sh_spec = pl.BlockSpec((nbb, RWKV_SHIFT_W), lambda i, j: (i, 0))
    perm, permt, tri = _rw_constants(nt, nbb, part)
    lw = lambda name: (wl[name], _layer_spec(wl[name].shape, l))
    const = lambda a: (a, _full_spec(a.shape))
    operands = (lw("g_pre"), lw("w_rw"), const(perm), const(permt), lw("rw_mu"), lw("rw_w0"), lw("rw_lora"),
                lw("rw_a0"), lw("rw_kk"), lw("rw_ka"), lw("rw_rk"), lw("rw_gnw"), lw("rw_gnb"),
                const(wl["rw_ones"]), const(tri))
    weights = [a for a, _ in operands]
    work = lambda dt: pltpu.VMEM((rows, BRANCH_W), dt)
    return pl.pallas_call(
        functools.partial(_rw_body, nt=nt, nbb=nbb, part=part, gw=gw),
        grid=grid,
        in_specs=[x_spec, st_in, sh_in] + [spec for _, spec in operands],
        out_specs=[y_spec, st_spec, sh_spec],
        out_shape=[y_shape, jax.ShapeDtypeStruct((batch, BRANCH_W, RWKV_HEAD), F32),
                   jax.ShapeDtypeStruct((batch, RWKV_SHIFT_W), F32)],
        scratch_shapes=[pltpu.VMEM((nbb, BRANCH_W // gw, gw, gw), F32),
                        pltpu.VMEM((nbb, BRANCH_W // gw, gw, gw), BF16),
                        pltpu.VMEM((nbb, RWKV_SHIFT_W), F32),
                        work(BF16), work(F32), work(BF16), work(BF16), work(BF16), work(F32), work(F32),
                        work(F32), work(F32)],
        compiler_params=_params(("arbitrary", "arbitrary")),
    )(x, s0, shift0, *weights)


def _merge_body(x_ref, ys5_ref, yrw_ref, ypl_ref, gpre_ref, wg_ref, wu_ref, wo_ref, gpost_ref, o_ref):
    x = x_ref[...]
    h = _rms_scale(x, gpre_ref[...]).astype(BF16)
    merged = None
    for i, y_ref in enumerate((ys5_ref, yrw_ref, ypl_ref)):
        gate = _sigmoid(_dot(h, wg_ref[:, D_MODEL * i:D_MODEL * (i + 1)]))
        term = gate * _dot(y_ref[...].astype(BF16), wu_ref[i])
        merged = term if merged is None else merged + term
    out = _dot(merged.astype(BF16), wo_ref[...])
    o_ref[...] = x + _rms_scale(out, gpost_ref[...])


def _merge_call(x, ys5, yrw, ypl, wl, *, l, tm):
    m = x.shape[0]
    weights = (wl["g_pre"], wl["w_gate"], wl["w_up"], wl["w_o"], wl["g_post"])
    row_spec = lambda w: pl.BlockSpec((tm, w), lambda i: (i, 0))
    return pl.pallas_call(
        _merge_body,
        grid=(m // tm,),
        in_specs=[row_spec(D_MODEL), row_spec(BRANCH_W), row_spec(BRANCH_W), row_spec(BRANCH_W)]
        + [_layer_spec(w.shape, l) for w in weights],
        out_specs=row_spec(D_MODEL),
        out_shape=jax.ShapeDtypeStruct((m, D_MODEL), F32),
        compiler_params=_params(("parallel",)),
    )(x, ys5, yrw, ypl, *weights)


def _block_diag(blocks):
    *lead, n, r, c = blocks.shape
    eye = jnp.eye(n, dtype=blocks.dtype)
    return jnp.einsum("...grc,gh->...grhc", blocks, eye).reshape(*lead, n * r, n * c)


def _prep_weights(norm_pre, norm_post, w_in, s5_lam_re, s5_lam_im, s5_log_dt, s5_b_re, s5_b_im,
                  s5_c_re, s5_c_im, s5_d, s5_w_glu, rwkv_mu, rwkv_w0, rwkv_w_w2, rwkv_a0, rwkv_w_a2,
                  rwkv_k_k, rwkv_k_a, rwkv_r_k, rwkv_gn_w, rwkv_gn_b, pool_w, pool_scale,
                  w_up_s5, w_up_rwkv, w_up_pool, w_o):
    depth = w_in.shape[0]
    row = lambda a: a.astype(F32).reshape(depth, 1, -1)
    wl = {}
    wl["g_pre"] = row(norm_pre)
    wl["g_post"] = row(norm_post)
    wl["w_s5"] = w_in[:, :, 0:1024].astype(BF16)
    wl["w_rw"] = w_in[:, :, 1024:3200].astype(BF16)
    wl["w_pool"] = w_in[:, :, 3200:4224].astype(BF16)
    wl["w_gate"] = w_in[:, :, 4224:7296].astype(BF16)

    lr = s5_lam_re.astype(F32)
    li = s5_lam_im.astype(F32)
    dt = jnp.exp(s5_log_dt.astype(F32))[:, :, None]
    mag = jnp.exp(lr * dt)
    ab_r = mag * jnp.cos(li * dt)
    ab_i = mag * jnp.sin(li * dt)
    den = lr * lr + li * li
    nr = ab_r - 1.0
    co_r = ((nr * lr + ab_i * li) / den)[..., None]
    co_i = ((ab_i * lr - nr * li) / den)[..., None]
    b_re = s5_b_re.astype(F32)
    b_im = s5_b_im.astype(F32)
    bf_re = co_r * b_re - co_i * b_im
    bf_im = co_r * b_im + co_i * b_re
    gh = S5_GROUPS // 2
    wb = []
    for k in range(2):
        gs = slice(gh * k, gh * (k + 1))
        wb.append(jnp.concatenate([_block_diag(jnp.swapaxes(bf_re[:, gs], 2, 3)),
                                   _block_diag(jnp.swapaxes(bf_im[:, gs], 2, 3))], axis=2))
    wl["s5_wb"] = jnp.stack(wb, axis=1).astype(BF16)
    wl["s5_ar"] = ab_r.reshape(depth, 1, S5_NSTATE)
    wl["s5_ai"] = ab_i.reshape(depth, 1, S5_NSTATE)
    c_re = s5_c_re.astype(F32)
    c_im = s5_c_im.astype(F32)
    cc = []
    for k in range(2):
        gs = slice(gh * k, gh * (k + 1))
        cc.append(jnp.concatenate([_block_diag(jnp.swapaxes(c_re[:, gs], 2, 3)),
                                   _block_diag(jnp.swapaxes(-c_im[:, gs], 2, 3))], axis=1))
    wl["s5_cc"] = jnp.stack(cc, axis=1).astype(BF16)
    wl["s5_d"] = row(s5_d)
    wl["s5_wglu"] = s5_w_glu.astype(BF16)

    wl["rw_mu"] = row(rwkv_mu)
    wl["rw_w0"] = row(rwkv_w0)
    zeros = jnp.zeros((depth, LORA_RANK, BRANCH_W), F32)
    wl["rw_lora"] = jnp.concatenate(
        [jnp.concatenate([rwkv_w_w2.astype(F32), zeros], axis=2),
         jnp.concatenate([zeros, rwkv_w_a2.astype(F32)], axis=2)], axis=1).astype(BF16)
    wl["rw_a0"] = row(rwkv_a0)
    wl["rw_kk"] = row(rwkv_k_k)
    wl["rw_ka"] = row(rwkv_k_a)
    wl["rw_rk"] = row(rwkv_r_k)
    wl["rw_gnw"] = row(rwkv_gn_w)
    wl["rw_gnb"] = row(rwkv_gn_b)
    wl["rw_ones"] = _block_diag(jnp.ones((MXU_TILE // RWKV_HEAD, RWKV_HEAD, RWKV_HEAD), F32)).astype(BF16)

    wl["pool_w"] = _block_diag(pool_w.astype(F32)).astype(BF16)
    wl["pool_scale"] = row(pool_scale)
    wl["w_up"] = jnp.stack([w_up_s5, w_up_rwkv, w_up_pool], axis=1).astype(BF16)
    wl["w_o"] = w_o.astype(BF16)
    return wl


def _layer(x, states, wl, *, l, sl, seq, batch, start, prompt):
    s5r, s5i, rws, rwsh, pbuf = states
    if prompt:
        nt, nb, nbb, y_dtype = 64, batch, batch, BF16
    else:
        nt, nb, nbb, y_dtype = seq, min(batch, 32), min(batch, 16), F32
    x3 = x.reshape(seq, batch, D_MODEL)
    flat = lambda y: y.reshape(seq * batch, BRANCH_W)
    if nb == batch:
        x_tm = x
        pbuf_in = pbuf.reshape(pbuf.shape[0], POOL_BUF * batch, BRANCH_W)
    else:
        x_tm = x3
        pbuf_in = pbuf
    ys5, n_re, n_im = _s5_call(x_tm, wl, s5r, s5i, l=l, sl=sl, seq=seq, batch=batch, nt=nt, nb=nb, y_dtype=y_dtype)
    ypl, n_buf = _pool_call(x_tm, wl, pbuf_in, l=l, sl=sl, seq=seq, batch=batch, nt=nt, nb=nb, start=start,
                            y_dtype=y_dtype)
    yrw, n_s, n_shift = _rw_call(x if nbb == batch else x3, rwsh, rws, wl, l=l, sl=sl, seq=seq, batch=batch,
                                 nt=nt, nbb=nbb, y_dtype=y_dtype)
    x_new = _merge_call(x, flat(ys5), flat(yrw), flat(ypl), wl, l=l, tm=512)
    n_buf = n_buf.reshape(POOL_BUF, batch, BRANCH_W)
    return x_new, (n_re, n_im, n_s, n_shift, n_buf)


def kernel(x_prompt, x_sample, state_s5_re, state_s5_im, state_rwkv, state_shift, state_pool, norm_pre, norm_post, w_in, s5_lam_re, s5_lam_im, s5_log_dt, s5_b_re, s5_b_im, s5_c_re, s5_c_im, s5_d, s5_w_glu, rwkv_mu, rwkv_w0, rwkv_w_w2, rwkv_a0, rwkv_w_a2, rwkv_k_k, rwkv_k_a, rwkv_r_k, rwkv_gn_w, rwkv_gn_b, pool_w, pool_scale, w_up_s5, w_up_rwkv, w_up_pool, w_o):
    depth = w_in.shape[0]
    bp, tp, _ = x_prompt.shape
    bs, ts, _ = x_sample.shape
    xp = jnp.transpose(x_prompt, (1, 0, 2)).reshape(tp * bp, D_MODEL)
    xs = jnp.transpose(x_sample, (1, 0, 2)).reshape(ts * bs, D_MODEL)
    outs_p = [[] for _ in range(5)]
    outs_s = [[] for _ in range(5)]
    wl = _prep_weights(norm_pre, norm_post, w_in, s5_lam_re, s5_lam_im, s5_log_dt, s5_b_re, s5_b_im,
                       s5_c_re, s5_c_im, s5_d, s5_w_glu, rwkv_mu, rwkv_w0, rwkv_w_w2, rwkv_a0, rwkv_w_a2,
                       rwkv_k_k, rwkv_k_a, rwkv_r_k, rwkv_gn_w, rwkv_gn_b, pool_w, pool_scale,
                       w_up_s5, w_up_rwkv, w_up_pool, w_o)
    zeros_p = (jnp.zeros((1, bp, S5_NSTATE), F32), jnp.zeros((1, bp, S5_NSTATE), F32),
               jnp.zeros((1, bp, BRANCH_W, RWKV_HEAD), F32), jnp.zeros((1, bp, RWKV_SHIFT_W), F32),
               jnp.zeros((1, POOL_BUF, bp, BRANCH_W), F32))
    st_s = (state_s5_re.reshape(depth, bs, S5_NSTATE), state_s5_im.reshape(depth, bs, S5_NSTATE),
            state_rwkv.reshape(depth, bs, BRANCH_W, RWKV_HEAD), state_shift,
            jnp.transpose(state_pool, (0, 2, 1, 3)))
    for l in range(depth):
        xp, new_p = _layer(xp, zeros_p, wl, l=l, sl=0, seq=tp, batch=bp, start=0, prompt=True)
        xs, new_s = _layer(xs, st_s, wl, l=l, sl=l, seq=ts, batch=bs, start=PAST_LEN, prompt=False)
        for lst, val in zip(outs_p, new_p):
            lst.append(val)
        for lst, val in zip(outs_s, new_s):
            lst.append(val)

    def finish(outs, b):
        re, im, rw, sh, pb = [jnp.stack(v) for v in outs]
        return (re.reshape(depth, b, S5_GROUPS, S5_STATE), im.reshape(depth, b, S5_GROUPS, S5_STATE),
                rw.reshape(depth, b, RWKV_HEADS, RWKV_HEAD, RWKV_HEAD), sh,
                jnp.transpose(pb, (0, 2, 1, 3)))

    y_p = jnp.transpose(xp.reshape(tp, bp, D_MODEL), (1, 0, 2))
    y_s = jnp.transpose(xs.reshape(ts, bs, D_MODEL), (1, 0, 2))
    return (y_p, y_s) + finish(outs_p, bp) + finish(outs_s, bs)
```

```python
import functools
import math

import jax
import jax.numpy as jnp
import numpy as np
from jax import lax
from jax.experimental import pallas as pl
from jax.experimental.pallas import tpu as pltpu

F32 = jnp.float32
BF16 = jnp.bfloat16

D_MODEL = 1024
BRANCH_W = 512
S5_GROUPS = 32
S5_GROUP = 16
S5_STATE = 64
S5_NSTATE = S5_GROUPS * S5_STATE
RWKV_HEADS = 8
RWKV_HEAD = 64
LORA_RANK = 64
RWKV_SHIFT_W = 3 * BRANCH_W + 2 * LORA_RANK
POOL_WINDOWS = (2, 4, 8, 16)
POOL_GW = 128
POOL_BUF = 15
NORM_EPS = 1e-6
GN_EPS = 64e-5
PAST_LEN = 16384

MXU_TILE = 256
LANES = 128
CHUNK = 64
RW_GROUP = 256
RW_UNROLL = 8
VMEM_LIMIT = 56 * 1024 * 1024


def _dot(a, b):
    return jnp.dot(a, b, preferred_element_type=F32)


def _dot_nt(a, b):
    return lax.dot_general(a, b, (((1,), (1,)), ((), ())), preferred_element_type=F32)


def _dot_tn(a, b):
    return lax.dot_general(a, b, (((0,), (0,)), ((), ())), preferred_element_type=F32)


def _split3(x):
    hi = x.astype(BF16)
    r1 = x - hi.astype(F32)
    mid = r1.astype(BF16)
    lo = (r1 - mid.astype(F32)).astype(BF16)
    return hi, mid, lo


def _rms_scale(x, g):
    ms = jnp.mean(x * x, axis=-1, keepdims=True)
    return x * lax.rsqrt(ms + NORM_EPS) * g


def _sigmoid(x):
    return jax.nn.sigmoid(x)


def _silu(x):
    return x * jax.nn.sigmoid(x)


def _gelu_tanh(x):
    c = math.sqrt(2.0 / math.pi)
    return 0.5 * x * (1.0 + jnp.tanh(c * (x + 0.044715 * (x * x * x))))


def _full_spec(shape):
    nd = len(shape)
    return pl.BlockSpec(shape, lambda *_: (0,) * nd)


def _layer_spec(shape, l):
    nd = len(shape) - 1
    return pl.BlockSpec((None,) + tuple(shape[1:]), lambda *_: (l,) + (0,) * nd)


def _params(sem):
    return pltpu.CompilerParams(dimension_semantics=sem, vmem_limit_bytes=VMEM_LIMIT)


def _s5_body(x_ref, g_ref, w_ref, wb_ref, ar_ref, ai_ref, cc_ref, d_ref, wglu_ref, h0r_ref, h0i_ref,
             y_ref, nr_ref, ni_ref, bbr, bbi, cr, ci, *, nt, nb):
    j = pl.program_id(1)
    rows = nt * nb
    half = S5_NSTATE // 2

    @pl.when(j == 0)
    def _():
        cr[...] = h0r_ref[...]
        ci[...] = h0i_ref[...]

    x = x_ref[...].reshape(rows, D_MODEL)
    h = _rms_scale(x, g_ref[...]).astype(BF16)
    sxz = _dot(h, w_ref[...])
    sx = sxz[:, :BRANCH_W]
    sz = sxz[:, BRANCH_W:]
    ub = sx.astype(BF16)
    for k in range(2):
        o = _dot(ub[:, MXU_TILE * k:MXU_TILE * (k + 1)], wb_ref[k])
        bbr[:, half * k:half * (k + 1)] = o[:, :half]
        bbi[:, half * k:half * (k + 1)] = o[:, half:]

    cw = max(128, min(1024, (8 * 1024) // nb))
    for c0 in range(0, S5_NSTATE, cw):
        cs = slice(c0, c0 + cw)
        a_r = jnp.broadcast_to(ar_ref[:, cs], (nb, cw))
        a_i = jnp.broadcast_to(ai_ref[:, cs], (nb, cw))

        def step(t, carry, cs=cs, a_r=a_r, a_i=a_i):
            hr, hi = carry
            rs = pl.ds(pl.multiple_of(t * nb, nb), nb)
            nhr = a_r * hr - a_i * hi + bbr[rs, cs]
            nhi = a_r * hi + a_i * hr + bbi[rs, cs]
            bbr[rs, cs] = nhr
            bbi[rs, cs] = nhi
            return nhr, nhi

        hr, hi = lax.fori_loop(0, nt, step, (cr[:, cs], ci[:, cs]), unroll=min(nt, 8))
        cr[:, cs] = hr
        ci[:, cs] = hi

    ys = []
    for k in range(2):
        ks = slice(half * k, half * (k + 1))
        lhs = jnp.concatenate([bbr[:, ks].astype(BF16), bbi[:, ks].astype(BF16)], axis=1)
        ys.append(_dot(lhs, cc_ref[k]))
    y = jnp.concatenate(ys, axis=1) + d_ref[...] * sx
    y = _gelu_tanh(y)
    y = y * _sigmoid(_dot(y.astype(BF16), wglu_ref[...]))
    y_ref[...] = (y * _silu(sz)).astype(y_ref.dtype).reshape(y_ref.shape)
    nr_ref[...] = cr[...]
    ni_ref[...] = ci[...]


def _s5_call(x, wl, h0r, h0i, *, l, sl, seq, batch, nt, nb, y_dtype):
    rows = nt * nb
    grid = (batch // nb, seq // nt)
    if nb == batch:
        x_spec = pl.BlockSpec((rows, D_MODEL), lambda i, j: (j, 0))
        y_spec = pl.BlockSpec((rows, BRANCH_W), lambda i, j: (j, 0))
        y_shape = jax.ShapeDtypeStruct((seq * batch, BRANCH_W), y_dtype)
    else:
        x_spec = pl.BlockSpec((nt, nb, D_MODEL), lambda i, j: (j, i, 0))
        y_spec = pl.BlockSpec((nt, nb, BRANCH_W), lambda i, j: (j, i, 0))
        y_shape = jax.ShapeDtypeStruct((seq, batch, BRANCH_W), y_dtype)
    st_in = pl.BlockSpec((None, nb, S5_NSTATE), lambda i, j: (sl, i, 0))
    st_spec = pl.BlockSpec((nb, S5_NSTATE), lambda i, j: (i, 0))
    st_shape = jax.ShapeDtypeStruct((batch, S5_NSTATE), F32)
    weights = (wl["g_pre"], wl["w_s5"], wl["s5_wb"], wl["s5_ar"], wl["s5_ai"], wl["s5_cc"], wl["s5_d"],
               wl["s5_wglu"])
    return pl.pallas_call(
        functools.partial(_s5_body, nt=nt, nb=nb),
        grid=grid,
        in_specs=[x_spec] + [_layer_spec(w.shape, l) for w in weights] + [st_in, st_in],
        out_specs=[y_spec, st_spec, st_spec],
        out_shape=[y_shape, st_shape, st_shape],
        scratch_shapes=[pltpu.VMEM((rows, S5_NSTATE), F32), pltpu.VMEM((rows, S5_NSTATE), F32),
                        pltpu.VMEM((nb, S5_NSTATE), F32), pltpu.VMEM((nb, S5_NSTATE), F32)],
        compiler_params=_params(("arbitrary", "arbitrary")),
    )(x, *weights, h0r, h0i)


def _pool_body(x_ref, g_ref, w_ref, pw_ref, ps_ref, buf_ref, y_ref, nbuf_ref, ext, *, nt, nb, start):
    j = pl.program_id(1)
    rows = nt * nb
    prev = POOL_BUF * nb

    @pl.when(j == 0)
    def _():
        ext[0:prev, :] = buf_ref[...].reshape(prev, BRANCH_W)

    x = x_ref[...].reshape(rows, D_MODEL)
    h = _rms_scale(x, g_ref[...]).astype(BF16)
    pxz = _dot(h, w_ref[...])
    px = pxz[:, :BRANCH_W]
    pz = pxz[:, BRANCH_W:]
    ext[prev:prev + rows, :] = px

    t_local = lax.broadcasted_iota(jnp.int32, (rows, POOL_GW), 0) // nb
    pos1 = start + j * nt + t_local + 1
    outs = []
    for gi, win in enumerate(POOL_WINDOWS):
        cs = slice(POOL_GW * gi, POOL_GW * (gi + 1))
        acc = ext[prev:prev + rows, cs]
        for q in range(1, win):
            acc = acc + ext[prev - q * nb:prev - q * nb + rows, cs]
        cnt = jnp.minimum(pos1, win).astype(F32)
        outs.append(acc / cnt)
    pooled = jnp.concatenate(outs, axis=1) - px
    mixed = _dot(pooled.astype(BF16), pw_ref[...]) * ps_ref[...]
    y_ref[...] = (mixed * _silu(pz)).astype(y_ref.dtype).reshape(y_ref.shape)
    last = ext[rows:rows + prev, :]
    nbuf_ref[...] = last.reshape(nbuf_ref.shape)
    ext[0:prev, :] = last


def _pool_call(x, wl, buf, *, l, sl, seq, batch, nt, nb, start, y_dtype):
    rows = nt * nb
    grid = (batch // nb, seq // nt)
    if nb == batch:
        x_spec = pl.BlockSpec((rows, D_MODEL), lambda i, j: (j, 0))
        y_spec = pl.BlockSpec((rows, BRANCH_W), lambda i, j: (j, 0))
        y_shape = jax.ShapeDtypeStruct((seq * batch, BRANCH_W), y_dtype)
        b_in = pl.BlockSpec((None, POOL_BUF * nb, BRANCH_W), lambda i, j: (sl, 0, 0))
        b_spec = pl.BlockSpec((POOL_BUF * nb, BRANCH_W), lambda i, j: (0, 0))
        b_shape = jax.ShapeDtypeStruct((POOL_BUF * batch, BRANCH_W), F32)
    else:
        x_spec = pl.BlockSpec((nt, nb, D_MODEL), lambda i, j: (j, i, 0))
        y_spec = pl.BlockSpec((nt, nb, BRANCH_W), lambda i, j: (j, i, 0))
        y_shape = jax.ShapeDtypeStruct((seq, batch, BRANCH_W), y_dtype)
        b_in = pl.BlockSpec((None, POOL_BUF, nb, BRANCH_W), lambda i, j: (sl, 0, i, 0))
        b_spec = pl.BlockSpec((POOL_BUF, nb, BRANCH_W), lambda i, j: (0, i, 0))
        b_shape = jax.ShapeDtypeStruct((POOL_BUF, batch, BRANCH_W), F32)
    weights = (wl["g_pre"], wl["w_pool"], wl["pool_w"], wl["pool_scale"])
    return pl.pallas_call(
        functools.partial(_pool_body, nt=nt, nb=nb, start=start),
        grid=grid,
        in_specs=[x_spec] + [_layer_spec(w.shape, l) for w in weights] + [b_in],
        out_specs=[y_spec, b_spec],
        out_shape=[y_shape, b_shape],
        scratch_shapes=[pltpu.VMEM(((POOL_BUF + nt) * nb, BRANCH_W), F32)],
        compiler_params=_params(("arbitrary", "arbitrary")),
    )(x, *weights, buf)


def _seg_sums(xs, ones):
    rows = xs[0].shape[0]
    stacked = jnp.concatenate([x.astype(BF16) for x in xs], axis=0)
    s = jnp.concatenate([_dot(stacked[:, MXU_TILE * g:MXU_TILE * (g + 1)], ones)
                         for g in range(BRANCH_W // MXU_TILE)], axis=1)
    return [s[i * rows:(i + 1) * rows] for i in range(len(xs))]


def _block_rows(y, head_masks):
    yb = y.astype(BF16)
    return jnp.concatenate([yb * m for m in head_masks], axis=0)


def _rw_body(x_ref, s0_ref, sh_ref, g_ref, w_ref, perm_ref, permt_ref, mu_ref, w0_ref, wl_ref, a0_ref,
             kkw_ref, kaw_ref, rkw_ref, gnw_ref, gnb_ref, ones_ref, tri_ref,
             y_ref, sout_ref, shout_ref,
             sbd, sbb, carry, qk_s, qr_s, kb_s, kk_s, v_s, ei_s, bonus_s, z_s, o_s,
             *, nt, nbb, part, gw):
    j = pl.program_id(1)
    rows = nt * nbb
    n_ch = rows // CHUNK
    spc = CHUNK // nt
    log_seq = nt.bit_length() - 1

    n_grp = BRANCH_W // gw
    grp_heads = gw // RWKV_HEAD

    lane = lax.broadcasted_iota(jnp.int32, (CHUNK, gw), 1)
    row64 = lax.broadcasted_iota(jnp.int32, (CHUNK, gw), 0)
    src = lane & (CHUNK - 1)
    same_seq = (row64 >> log_seq) == (src >> log_seq)
    m_strict = same_seq & (src < row64)
    m_incl = same_seq & (src <= row64)
    eye_sbs = (src == row64).astype(F32)
    head_masks = [((lane >> 6) == hh).astype(BF16) for hh in range(grp_heads)]
    half_lane = lax.broadcasted_iota(jnp.int32, (RWKV_HEAD, LANES), 1)
    half_masks = [half_lane < RWKV_HEAD, half_lane >= RWKV_HEAD]

    @pl.when(j == 0)
    def _():
        zero_blk = jnp.zeros((RWKV_HEAD, RWKV_HEAD), F32)
        for q in range(nbb):
            for g in range(n_grp):
                sbd[q, g] = jnp.zeros((gw, gw), F32)
                sbb[q, g] = jnp.zeros((gw, gw), BF16)
                for hh in range(grp_heads):
                    hr = slice(RWKV_HEAD * hh, RWKV_HEAD * (hh + 1))
                    hl = slice(LANES * (hh // 2), LANES * (hh // 2 + 1))
                    blk = s0_ref[q, gw * g + RWKV_HEAD * hh:gw * g + RWKV_HEAD * (hh + 1), :]
                    piece = jnp.concatenate([blk, zero_blk] if hh % 2 == 0 else [zero_blk, blk], axis=1)
                    sbd[q, g, hr, hl] = piece
                    sbb[q, g, hr, hl] = piece.astype(BF16)
        carry[...] = sh_ref[...]

    x = x_ref[...].reshape(rows, D_MODEL)
    h = _rms_scale(x, g_ref[...]).astype(BF16)
    hb = _dot(perm_ref[...], h).astype(BF16)
    ones = ones_ref[...]

    is_first = (lax.broadcasted_iota(jnp.int32, (part, RWKV_SHIFT_W), 0) & (nt - 1)) == 0
    lane128 = lax.broadcasted_iota(jnp.int32, (part, 2 * LORA_RANK), 1)

    def prepare(r0):
        rs = slice(r0, r0 + part)
        q0 = r0 // nt
        pz = _dot(hb[rs], w_ref[...])
        yield
        p = pz[:, :RWKV_SHIFT_W]
        z_s[rs, :] = pz[:, RWKV_SHIFT_W:]
        first = jnp.concatenate(
            [jnp.broadcast_to(carry[q0 + q:q0 + q + 1, :], (nt, RWKV_SHIFT_W)) for q in range(part // nt)], axis=0)
        prev = jnp.where(is_first, first, pltpu.roll(p, 1, 0))
        for q in range(part // nt):
            carry[q0 + q:q0 + q + 1, :] = p[nt * (q + 1) - 1:nt * (q + 1), :]
        pm = p + (prev - p) * mu_ref[...]
        yield
        r = pm[:, 0:BRANCH_W]
        k = pm[:, BRANCH_W:2 * BRANCH_W]
        v = pm[:, 2 * BRANCH_W:3 * BRANCH_W]
        ll = pm[:, 3 * BRANCH_W:]
        xl = jnp.where(lane128 < LORA_RANK, jnp.tanh(ll), ll).astype(BF16)
        dl = _dot(xl, wl_ref[...])
        yield
        w_log = -jax.nn.softplus(-(w0_ref[...] + dl[:, :BRANCH_W])) - 0.5
        ld = -jnp.exp(w_log)
        yield
        a = _sigmoid(a0_ref[...] + dl[:, BRANCH_W:])
        kk = k * kkw_ref[...]
        k2 = k * (1.0 + (a - 1.0) * kaw_ref[...])
        yield
        ssq, rk = _seg_sums([kk * kk, r * k2 * rkw_ref[...]], ones)
        hi, mid, lo = _split3(ld)
        cum3 = _dot(tri_ref[...], jnp.concatenate([hi, mid, lo], axis=1))
        yield
        kkn = kk * lax.rsqrt(jnp.maximum(ssq, 1e-24))
        cum = cum3[:, :BRANCH_W] + cum3[:, BRANCH_W:2 * BRANCH_W] + cum3[:, 2 * BRANCH_W:]
        e_incl = jnp.exp(cum)
        yield
        e_inv = jnp.exp(-cum)
        yield
        qk_s[rs, :] = (kkn * jnp.exp(cum - ld)).astype(BF16)
        qr_s[rs, :] = r * e_incl
        yield
        kb_s[rs, :] = (kkn * a * e_inv).astype(BF16)
        kk_s[rs, :] = (k2 * e_inv).astype(BF16)
        v_s[rs, :] = v.astype(BF16)
        ei_s[rs, :] = e_incl
        bonus_s[rs, :] = rk * v

    def chunk(c, g):
        r0 = c * CHUNK if isinstance(c, int) else pl.multiple_of(c * CHUNK, CHUNK)
        rs = pl.ds(r0, CHUNK)
        ls = slice(gw * g, gw * (g + 1))
        q_k = qk_s[rs, ls]
        q_r = qr_s[rs, ls]
        k_b = kb_s[rs, ls]
        k_k = kk_s[rs, ls]
        vb = v_s[rs, ls]
        q2 = jnp.concatenate([q_k, q_r.astype(BF16)], axis=0)
        bd_v = _block_rows(vb, head_masks)
        a_all = _dot_nt(q2, jnp.concatenate([_block_rows(k_b, head_masks), _block_rows(k_k, head_masks)], axis=0))
        yield
        a_kb = jnp.where(m_strict, a_all[:CHUNK, :gw], 0.0)
        a_rb = jnp.where(m_incl, a_all[CHUNK:, :gw], 0.0).astype(BF16)
        a_kk = jnp.where(m_strict, a_all[:CHUNK, gw:], 0.0).astype(BF16)
        a_rk = jnp.where(m_incl, a_all[CHUNK:, gw:], 0.0).astype(BF16)
        av = _dot(jnp.concatenate([a_kk, a_rk], axis=0), bd_v)
        npow = -a_kb
        tinv = eye_sbs + npow
        npow = _dot(npow.astype(BF16), _block_rows(npow, head_masks))
        yield
        akv = av[:CHUNK]
        o_v = av[CHUNK:]
        for it in range(1, log_seq):
            bd_n = _block_rows(npow, head_masks)
            if it + 1 < log_seq:
                both = _dot(jnp.concatenate([npow.astype(BF16), tinv.astype(BF16)], axis=0), bd_n)
                yield
                npow = both[:CHUNK]
                tinv = tinv + both[CHUNK:]
            else:
                last = _dot(tinv.astype(BF16), bd_n)
                yield
                tinv = tinv + last
        tw = _dot(tinv.astype(BF16),
                  jnp.concatenate([_block_rows(q_k, head_masks), _block_rows(-akv, head_masks)], axis=1))
        yield
        qt_k = tw[:, :gw]
        w_loc = tw[:, gw:]
        d1 = _dot(a_rb, jnp.concatenate([_block_rows(qt_k, head_masks), _block_rows(w_loc, head_masks)], axis=1))
        yield
        qh_r = q_r - d1[:, :gw]
        o_loc = d1[:, gw:] + o_v

        def update_state(sq, zz, pc):
            for hh in range(grp_heads):
                hr = slice(RWKV_HEAD * hh, RWKV_HEAD * (hh + 1))
                hl = slice(LANES * (hh // 2), LANES * (hh // 2 + 1))
                piece = (sbd[sq, g, hr, hl] + jnp.where(half_masks[hh % 2], zz[hr, hl], 0.0)) * pc[:, hl]
                sbd[sq, g, hr, hl] = piece
                sbb[sq, g, hr, hl] = piece.astype(BF16)

        if spc == 1:
            xs = _dot_nt(jnp.concatenate([qt_k.astype(BF16), qh_r.astype(BF16)], axis=0), sbb[c, g])
            yield
            u = w_loc - xs[:CHUNK]
            o_s[rs, ls] = xs[CHUNK:] + o_loc
            zz = _dot_tn(jnp.concatenate([u.astype(BF16), vb], axis=0), jnp.concatenate([k_b, k_k], axis=0))
            yield
            update_state(c, zz, ei_s[pl.ds(r0 + CHUNK - 1, 1), ls])
        else:
            kb_f = k_b.astype(F32)
            kk_f = k_k.astype(F32)
            v_f = vb.astype(F32)
            seqs = [slice(nt * q, nt * (q + 1)) for q in range(spc)]
            xs = [_dot_nt(jnp.concatenate([qt_k[qs], qh_r[qs]], axis=0).astype(BF16), sbb[c * spc + q, g])
                  for q, qs in enumerate(seqs)]
            yield
            o_s[rs, ls] = jnp.concatenate([x[nt:] for x in xs], axis=0) + o_loc
            zz = [_dot_tn(jnp.concatenate([w_loc[qs] - x[:nt], v_f[qs]], axis=0).astype(BF16),
                          jnp.concatenate([kb_f[qs], kk_f[qs]], axis=0).astype(BF16))
                  for x, qs in zip(xs, seqs)]
            yield
            for q in range(spc):
                update_state(c * spc + q, zz[q], ei_s[pl.ds(r0 + nt * (q + 1) - 1, 1), ls])

    def run_together(live):
        while live:
            still = []
            for gen in live:
                try:
                    next(gen)
                    still.append(gen)
                except StopIteration:
                    pass
            live = still

    run_together([prepare(r0) for r0 in range(0, rows, part)])
    if n_ch <= RW_UNROLL:
        run_together([chunk(c, g) for c in range(n_ch) for g in range(n_grp)])
    else:
        def loop_body(cc, carry_):
            run_together([chunk(cc * RW_UNROLL + i, g) for i in range(RW_UNROLL) for g in range(n_grp)])
            return carry_
        lax.fori_loop(0, n_ch // RW_UNROLL, loop_body, 0)

    ybs = []
    for r0 in range(0, rows, part):
        rs = slice(r0, r0 + part)
        o = o_s[rs, :]
        mean = _seg_sums([o], ones)[0] * (1.0 / RWKV_HEAD)
        dev = o - mean
        var = _seg_sums([dev * dev], ones)[0] * (1.0 / RWKV_HEAD)
        o = dev * lax.rsqrt(var + GN_EPS) * gnw_ref[...] + gnb_ref[...] + bonus_s[rs, :]
        ybs.append((o * _silu(z_s[rs, :])).astype(BF16))
    y_tm = _dot(permt_ref[...], jnp.concatenate(ybs, axis=0))
    y_ref[...] = y_tm.astype(y_ref.dtype).reshape(y_ref.shape)
    shout_ref[...] = carry[...]

    @pl.when(j == pl.num_programs(1) - 1)
    def _():
        for q in range(nbb):
            for g in range(n_grp):
                for hh in range(grp_heads):
                    hr = slice(RWKV_HEAD * hh, RWKV_HEAD * (hh + 1))
                    hl = slice(RWKV_HEAD * hh, RWKV_HEAD * (hh + 1))
                    sout_ref[q, gw * g + RWKV_HEAD * hh:gw * g + RWKV_HEAD * (hh + 1), :] = sbd[q, g, hr, hl]


def _rw_constants(nt, nbb, part):
    rows = nt * nbb
    r = np.arange(rows)
    src = (r % nt) * nbb + (r // nt)
    perm = np.zeros((rows, rows), np.float32)
    perm[r, src] = 1.0
    t = np.arange(part)
    tri = ((t[:, None] // nt) == (t[None, :] // nt)) & (t[None, :] <= t[:, None])
    return jnp.asarray(perm, BF16), jnp.asarray(perm.T, BF16), jnp.asarray(tri, BF16)


def _rw_call(x, shift0, s0, wl, *, l, sl, seq, batch, nt, nbb, y_dtype):
    rows = nt * nbb
    part = min(rows, 256)
    gw = RW_GROUP if nt == CHUNK else MXU_TILE
    grid = (batch // nbb, seq // nt)
    if nbb == batch:
        x_spec = pl.BlockSpec((rows, D_MODEL), lambda i, j: (j, 0))
        y_spec = pl.BlockSpec((rows, BRANCH_W), lambda i, j: (j, 0))
        y_shape = jax.ShapeDtypeStruct((seq * batch, BRANCH_W), y_dtype)
    else:
        x_spec = pl.BlockSpec((nt, nbb, D_MODEL), lambda i, j: (j, i, 0))
        y_spec = pl.BlockSpec((nt, nbb, BRANCH_W), lambda i, j: (j, i, 0))
        y_shape = jax.ShapeDtypeStruct((seq, batch, BRANCH_W), y_dtype)
    st_in = pl.BlockSpec((None, nbb, BRANCH_W, RWKV_HEAD), lambda i, j: (sl, i, 0, 0))
    sh_in = pl.BlockSpec((None, nbb, RWKV_SHIFT_W), lambda i, j: (sl, i, 0))
    st_spec = pl.BlockSpec((nbb, BRANCH_W, RWKV_HEAD), lambda i, j: (i, 0, 0))
    sh_spec = pl.BlockSpec((nbb, RWKV_SHIFT_W), lambda i, j: (i, 0))
    perm, permt, tri = _rw_constants(nt, nbb, part)
    lw = lambda name: (wl[name], _layer_spec(wl[name].shape, l))
    const = lambda a: (a, _full_spec(a.shape))
    operands = (lw("g_pre"), lw("w_rw"), const(perm), const(permt), lw("rw_mu"), lw("rw_w0"), lw("rw_lora"),
                lw("rw_a0"), lw("rw_kk"), lw("rw_ka"), lw("rw_rk"), lw("rw_gnw"), lw("rw_gnb"),
                const(wl["rw_ones"]), const(tri))
    weights = [a for a, _ in operands]
    work = lambda dt: pltpu.VMEM((rows, BRANCH_W), dt)
    return pl.pallas_call(
        functools.partial(_rw_body, nt=nt, nbb=nbb, part=part, gw=gw),
        grid=grid,
        in_specs=[x_spec, st_in, sh_in] + [spec for _, spec in operands],
        out_specs=[y_spec, st_spec, sh_spec],
        out_shape=[y_shape, jax.ShapeDtypeStruct((batch, BRANCH_W, RWKV_HEAD), F32),
                   jax.ShapeDtypeStruct((batch, RWKV_SHIFT_W), F32)],
        scratch_shapes=[pltpu.VMEM((nbb, BRANCH_W // gw, gw, gw), F32),
                        pltpu.VMEM((nbb, BRANCH_W // gw, gw, gw), BF16),
                        pltpu.VMEM((nbb, RWKV_SHIFT_W), F32),
                        work(BF16), work(F32), work(BF16), work(BF16), work(BF16), work(F32), work(F32),
                        work(F32), work(F32)],
        compiler_params=_params(("arbitrary", "arbitrary")),
    )(x, s0, shift0, *weights)


def _merge_body(x_ref, ys5_ref, yrw_ref, ypl_ref, gpre_ref, wg_ref, wu_ref, wo_ref, gpost_ref, o_ref):
    x = x_ref[...]
    h = _rms_scale(x, gpre_ref[...]).astype(BF16)
    merged = None
    for i, y_ref in enumerate((ys5_ref, yrw_ref, ypl_ref)):
        gate = _sigmoid(_dot(h, wg_ref[:, D_MODEL * i:D_MODEL * (i + 1)]))
        term = gate * _dot(y_ref[...].astype(BF16), wu_ref[i])
        merged = term if merged is None else merged + term
    out = _dot(merged.astype(BF16), wo_ref[...])
    o_ref[...] = x + _rms_scale(out, gpost_ref[...])


def _merge_call(x, ys5, yrw, ypl, wl, *, l, tm):
    m = x.shape[0]
    tm = min(tm, m)
    weights = (wl["g_pre"], wl["w_gate"], wl["w_up"], wl["w_o"], wl["g_post"])
    row_spec = lambda w: pl.BlockSpec((tm, w), lambda i: (i, 0))
    return pl.pallas_call(
        _merge_body,
        grid=(m // tm,),
        in_specs=[row_spec(D_MODEL), row_spec(BRANCH_W), row_spec(BRANCH_W), row_spec(BRANCH_W)]
        + [_layer_spec(w.shape, l) for w in weights],
        out_specs=row_spec(D_MODEL),
        out_shape=jax.ShapeDtypeStruct((m, D_MODEL), F32),
        compiler_params=_params(("parallel",)),
    )(x, ys5, yrw, ypl, *weights)


def _block_diag(blocks):
    *lead, n, r, c = blocks.shape
    eye = jnp.eye(n, dtype=blocks.dtype)
    return jnp.einsum("...grc,gh->...grhc", blocks, eye).reshape(*lead, n * r, n * c)


def _prep_weights(norm_pre, norm_post, w_in, s5_lam_re, s5_lam_im, s5_log_dt, s5_b_re, s5_b_im,
                  s5_c_re, s5_c_im, s5_d, s5_w_glu, rwkv_mu, rwkv_w0, rwkv_w_w2, rwkv_a0, rwkv_w_a2,
                  rwkv_k_k, rwkv_k_a, rwkv_r_k, rwkv_gn_w, rwkv_gn_b, pool_w, pool_scale,
                  w_up_s5, w_up_rwkv, w_up_pool, w_o):
    depth = w_in.shape[0]
    row = lambda a: a.astype(F32).reshape(depth, 1, -1)
    wl = {}
    wl["g_pre"] = row(norm_pre)
    wl["g_post"] = row(norm_post)
    wl["w_s5"] = w_in[:, :, 0:1024].astype(BF16)
    wl["w_rw"] = w_in[:, :, 1024:3200].astype(BF16)
    wl["w_pool"] = w_in[:, :, 3200:4224].astype(BF16)
    wl["w_gate"] = w_in[:, :, 4224:7296].astype(BF16)

    lr = s5_lam_re.astype(F32)
    li = s5_lam_im.astype(F32)
    dt = jnp.exp(s5_log_dt.astype(F32))[:, :, None]
    mag = jnp.exp(lr * dt)
    ab_r = mag * jnp.cos(li * dt)
    ab_i = mag * jnp.sin(li * dt)
    den = lr * lr + li * li
    nr = ab_r - 1.0
    co_r = ((nr * lr + ab_i * li) / den)[..., None]
    co_i = ((ab_i * lr - nr * li) / den)[..., None]
    b_re = s5_b_re.astype(F32)
    b_im = s5_b_im.astype(F32)
    bf_re = co_r * b_re - co_i * b_im
    bf_im = co_r * b_im + co_i * b_re
    gh = S5_GROUPS // 2
    wb = []
    for k in range(2):
        gs = slice(gh * k, gh * (k + 1))
        wb.append(jnp.concatenate([_block_diag(jnp.swapaxes(bf_re[:, gs], 2, 3)),
                                   _block_diag(jnp.swapaxes(bf_im[:, gs], 2, 3))], axis=2))
    wl["s5_wb"] = jnp.stack(wb, axis=1).astype(BF16)
    wl["s5_ar"] = ab_r.reshape(depth, 1, S5_NSTATE)
    wl["s5_ai"] = ab_i.reshape(depth, 1, S5_NSTATE)
    c_re = s5_c_re.astype(F32)
    c_im = s5_c_im.astype(F32)
    cc = []
    for k in range(2):
        gs = slice(gh * k, gh * (k + 1))
        cc.append(jnp.concatenate([_block_diag(jnp.swapaxes(c_re[:, gs], 2, 3)),
                                   _block_diag(jnp.swapaxes(-c_im[:, gs], 2, 3))], axis=1))
    wl["s5_cc"] = jnp.stack(cc, axis=1).astype(BF16)
    wl["s5_d"] = row(s5_d)
    wl["s5_wglu"] = s5_w_glu.astype(BF16)

    wl["rw_mu"] = row(rwkv_mu)
    wl["rw_w0"] = row(rwkv_w0)
    zeros = jnp.zeros((depth, LORA_RANK, BRANCH_W), F32)
    wl["rw_lora"] = jnp.concatenate(
        [jnp.concatenate([rwkv_w_w2.astype(F32), zeros], axis=2),
         jnp.concatenate([zeros, rwkv_w_a2.astype(F32)], axis=2)], axis=1).astype(BF16)
    wl["rw_a0"] = row(rwkv_a0)
    wl["rw_kk"] = row(rwkv_k_k)
    wl["rw_ka"] = row(rwkv_k_a)
    wl["rw_rk"] = row(rwkv_r_k)
    wl["rw_gnw"] = row(rwkv_gn_w)
    wl["rw_gnb"] = row(rwkv_gn_b)
    wl["rw_ones"] = _block_diag(jnp.ones((MXU_TILE // RWKV_HEAD, RWKV_HEAD, RWKV_HEAD), F32)).astype(BF16)

    wl["pool_w"] = _block_diag(pool_w.astype(F32)).astype(BF16)
    wl["pool_scale"] = row(pool_scale)
    wl["w_up"] = jnp.stack([w_up_s5, w_up_rwkv, w_up_pool], axis=1).astype(BF16)
    wl["w_o"] = w_o.astype(BF16)
    return wl


def _layer(x, states, wl, *, l, sl, seq, batch, start, prompt):
    s5r, s5i, rws, rwsh, pbuf = states
    if prompt:
        nt, nb, nbb, y_dtype = CHUNK, batch, batch, BF16
        nts = min(seq, 2 * CHUNK)
    else:
        nt, nb, nbb, y_dtype = seq, min(batch, 32), min(batch, 16), F32
        nts = seq
    x3 = x.reshape(seq, batch, D_MODEL)
    flat = lambda y: y.reshape(seq * batch, BRANCH_W)
    if nb == batch:
        x_tm = x
        pbuf_in = pbuf.reshape(pbuf.shape[0], POOL_BUF * batch, BRANCH_W)
    else:
        x_tm = x3
        pbuf_in = pbuf
    ys5, n_re, n_im = _s5_call(x_tm, wl, s5r, s5i, l=l, sl=sl, seq=seq, batch=batch, nt=nts, nb=nb, y_dtype=y_dtype)
    ypl, n_buf = _pool_call(x_tm, wl, pbuf_in, l=l, sl=sl, seq=seq, batch=batch, nt=nts, nb=nb, start=start,
                            y_dtype=y_dtype)
    yrw, n_s, n_shift = _rw_call(x if nbb == batch else x3, rwsh, rws, wl, l=l, sl=sl, seq=seq, batch=batch,
                                 nt=nt, nbb=nbb, y_dtype=y_dtype)
    x_new = _merge_call(x, flat(ys5), flat(yrw), flat(ypl), wl, l=l, tm=1024)
    n_buf = n_buf.reshape(POOL_BUF, batch, BRANCH_W)
    return x_new, (n_re, n_im, n_s, n_shift, n_buf)


def kernel(x_prompt, x_sample, state_s5_re, state_s5_im, state_rwkv, state_shift, state_pool, norm_pre, norm_post, w_in, s5_lam_re, s5_lam_im, s5_log_dt, s5_b_re, s5_b_im, s5_c_re, s5_c_im, s5_d, s5_w_glu, rwkv_mu, rwkv_w0, rwkv_w_w2, rwkv_a0, rwkv_w_a2, rwkv_k_k, rwkv_k_a, rwkv_r_k, rwkv_gn_w, rwkv_gn_b, pool_w, pool_scale, w_up_s5, w_up_rwkv, w_up_pool, w_o):
    depth = w_in.shape[0]
    bp, tp, _ = x_prompt.shape
    bs, ts, _ = x_sample.shape
    xp = jnp.transpose(x_prompt, (1, 0, 2)).reshape(tp * bp, D_MODEL)
    xs = jnp.transpose(x_sample, (1, 0, 2)).reshape(ts * bs, D_MODEL)
    outs_p = [[] for _ in range(5)]
    outs_s = [[] for _ in range(5)]
    wl = _prep_weights(norm_pre, norm_post, w_in, s5_lam_re, s5_lam_im, s5_log_dt, s5_b_re, s5_b_im,
                       s5_c_re, s5_c_im, s5_d, s5_w_glu, rwkv_mu, rwkv_w0, rwkv_w_w2, rwkv_a0, rwkv_w_a2,
                       rwkv_k_k, rwkv_k_a, rwkv_r_k, rwkv_gn_w, rwkv_gn_b, pool_w, pool_scale,
                       w_up_s5, w_up_rwkv, w_up_pool, w_o)
    zeros_p = (jnp.zeros((1, bp, S5_NSTATE), F32), jnp.zeros((1, bp, S5_NSTATE), F32),
               jnp.zeros((1, bp, BRANCH_W, RWKV_HEAD), F32), jnp.zeros((1, bp, RWKV_SHIFT_W), F32),
               jnp.zeros((1, POOL_BUF, bp, BRANCH_W), F32))
    st_s = (state_s5_re.reshape(depth, bs, S5_NSTATE), state_s5_im.reshape(depth, bs, S5_NSTATE),
            state_rwkv.reshape(depth, bs, BRANCH_W, RWKV_HEAD), state_shift,
            jnp.transpose(state_pool, (0, 2, 1, 3)))
    for l in range(depth):
        xp, new_p = _layer(xp, zeros_p, wl, l=l, sl=0, seq=tp, batch=bp, start=0, prompt=True)
        xs, new_s = _layer(xs, st_s, wl, l=l, sl=l, seq=ts, batch=bs, start=PAST_LEN, prompt=False)
        for lst, val in zip(outs_p, new_p):
            lst.append(val)
        for lst, val in zip(outs_s, new_s):
            lst.append(val)

    def finish(outs, b):
        re, im, rw, sh, pb = [jnp.stack(v) for v in outs]
        return (re.reshape(depth, b, S5_GROUPS, S5_STATE), im.reshape(depth, b, S5_GROUPS, S5_STATE),
                rw.reshape(depth, b, RWKV_HEADS, RWKV_HEAD, RWKV_HEAD), sh,
                jnp.transpose(pb, (0, 2, 1, 3)))

    y_p = jnp.transpose(xp.reshape(tp, bp, D_MODEL), (1, 0, 2))
    y_s = jnp.transpose(xs.reshape(ts, bs, D_MODEL), (1, 0, 2))
    return (y_p, y_s) + finish(outs_p, bp) + finish(outs_s, bs)
```

```python
import functools
import math

import jax
import jax.numpy as jnp
import numpy as np
from jax import lax
from jax.experimental import pallas as pl
from jax.experimental.pallas import tpu as pltpu

F32 = jnp.float32
BF16 = jnp.bfloat16

D_MODEL = 1024
BRANCH_W = 512
S5_GROUPS = 32
S5_GROUP = 16
S5_STATE = 64
S5_NSTATE = S5_GROUPS * S5_STATE
RWKV_HEADS = 8
RWKV_HEAD = 64
LORA_RANK = 64
RWKV_SHIFT_W = 3 * BRANCH_W + 2 * LORA_RANK
POOL_WINDOWS = (2, 4, 8, 16)
POOL_GW = 128
POOL_BUF = 15
NORM_EPS = 1e-6
GN_EPS = 64e-5
PAST_LEN = 16384

MXU_TILE = 256
LANES = 128
CHUNK = 64
RW_GROUP = 256
RW_UNROLL = 8
VMEM_LIMIT = 56 * 1024 * 1024


def _dot(a, b):
    return jnp.dot(a, b, preferred_element_type=F32)


def _dot_nt(a, b):
    return lax.dot_general(a, b, (((1,), (1,)), ((), ())), preferred_element_type=F32)


def _dot_tn(a, b):
    return lax.dot_general(a, b, (((0,), (0,)), ((), ())), preferred_element_type=F32)


def _split3(x):
    hi = x.astype(BF16)
    r1 = x - hi.astype(F32)
    mid = r1.astype(BF16)
    lo = (r1 - mid.astype(F32)).astype(BF16)
    return hi, mid, lo


def _rms_scale(x, g):
    ms = jnp.mean(x * x, axis=-1, keepdims=True)
    return x * lax.rsqrt(ms + NORM_EPS) * g


def _sigmoid(x):
    return jax.nn.sigmoid(x)


def _silu(x):
    return x * jax.nn.sigmoid(x)


def _gelu_tanh(x):
    c = math.sqrt(2.0 / math.pi)
    return 0.5 * x * (1.0 + jnp.tanh(c * (x + 0.044715 * (x * x * x))))


def _full_spec(shape):
    nd = len(shape)
    return pl.BlockSpec(shape, lambda *_: (0,) * nd)


def _layer_spec(shape, l):
    nd = len(shape) - 1
    return pl.BlockSpec((None,) + tuple(shape[1:]), lambda *_: (l,) + (0,) * nd)


def _weight_operand(w, l):
    return w, _layer_spec(w.shape, l)


def _params(sem):
    return pltpu.CompilerParams(dimension_semantics=sem, vmem_limit_bytes=VMEM_LIMIT)


def _s5_body(x_ref, g_ref, w_ref, wb_ref, ar_ref, ai_ref, cc_ref, d_ref, wglu_ref, h0r_ref, h0i_ref,
             y_ref, nr_ref, ni_ref, bbr, bbi, cr, ci, *, nt, nb):
    j = pl.program_id(1)
    rows = nt * nb
    half = S5_NSTATE // 2

    @pl.when(j == 0)
    def _():
        cr[...] = h0r_ref[...]
        ci[...] = h0i_ref[...]

    x = x_ref[...].reshape(rows, D_MODEL)
    h = _rms_scale(x, g_ref[...]).astype(BF16)
    sxz = _dot(h, w_ref[...])
    sx = sxz[:, :BRANCH_W]
    sz = sxz[:, BRANCH_W:]
    ub = sx.astype(BF16)
    for k in range(2):
        o = _dot(ub[:, MXU_TILE * k:MXU_TILE * (k + 1)], wb_ref[k])
        bbr[:, half * k:half * (k + 1)] = o[:, :half]
        bbi[:, half * k:half * (k + 1)] = o[:, half:]

    cw = max(128, min(1024, (8 * 1024) // nb))
    for c0 in range(0, S5_NSTATE, cw):
        cs = slice(c0, c0 + cw)
        a_r = jnp.broadcast_to(ar_ref[:, cs], (nb, cw))
        a_i = jnp.broadcast_to(ai_ref[:, cs], (nb, cw))

        def step(t, carry, cs=cs, a_r=a_r, a_i=a_i):
            hr, hi = carry
            rs = pl.ds(pl.multiple_of(t * nb, nb), nb)
            nhr = a_r * hr - a_i * hi + bbr[rs, cs]
            nhi = a_r * hi + a_i * hr + bbi[rs, cs]
            bbr[rs, cs] = nhr
            bbi[rs, cs] = nhi
            return nhr, nhi

        hr, hi = lax.fori_loop(0, nt, step, (cr[:, cs], ci[:, cs]), unroll=min(nt, 8))
        cr[:, cs] = hr
        ci[:, cs] = hi

    ys = []
    for k in range(2):
        ks = slice(half * k, half * (k + 1))
        lhs = jnp.concatenate([bbr[:, ks].astype(BF16), bbi[:, ks].astype(BF16)], axis=1)
        ys.append(_dot(lhs, cc_ref[k]))
    y = jnp.concatenate(ys, axis=1) + d_ref[...] * sx
    y = _gelu_tanh(y)
    y = y * _sigmoid(_dot(y.astype(BF16), wglu_ref[...]))
    y_ref[...] = (y * _silu(sz)).astype(y_ref.dtype).reshape(y_ref.shape)
    nr_ref[...] = cr[...]
    ni_ref[...] = ci[...]


def _s5_call(x, wl, h0r, h0i, *, l, sl, seq, batch, nt, nb, y_dtype):
    rows = nt * nb
    grid = (batch // nb, seq // nt)
    if nb == batch:
        x_spec = pl.BlockSpec((rows, D_MODEL), lambda i, j: (j, 0))
        y_spec = pl.BlockSpec((rows, BRANCH_W), lambda i, j: (j, 0))
        y_shape = jax.ShapeDtypeStruct((seq * batch, BRANCH_W), y_dtype)
    else:
        x_spec = pl.BlockSpec((nt, nb, D_MODEL), lambda i, j: (j, i, 0))
        y_spec = pl.BlockSpec((nt, nb, BRANCH_W), lambda i, j: (j, i, 0))
        y_shape = jax.ShapeDtypeStruct((seq, batch, BRANCH_W), y_dtype)
    st_in = pl.BlockSpec((None, nb, S5_NSTATE), lambda i, j: (sl, i, 0))
    st_spec = pl.BlockSpec((nb, S5_NSTATE), lambda i, j: (i, 0))
    st_shape = jax.ShapeDtypeStruct((batch, S5_NSTATE), F32)
    weights, w_specs = zip(*[_weight_operand(wl[k], l) for k in (
        "g_pre", "w_s5", "s5_wb", "s5_ar", "s5_ai", "s5_cc", "s5_d", "s5_wglu")])
    return pl.pallas_call(
        functools.partial(_s5_body, nt=nt, nb=nb),
        grid=grid,
        in_specs=[x_spec] + list(w_specs) + [st_in, st_in],
        out_specs=[y_spec, st_spec, st_spec],
        out_shape=[y_shape, st_shape, st_shape],
        scratch_shapes=[pltpu.VMEM((rows, S5_NSTATE), F32), pltpu.VMEM((rows, S5_NSTATE), F32),
                        pltpu.VMEM((nb, S5_NSTATE), F32), pltpu.VMEM((nb, S5_NSTATE), F32)],
        compiler_params=_params(("arbitrary", "arbitrary")),
    )(x, *weights, h0r, h0i)


def _pool_body(x_ref, g_ref, w_ref, pw_ref, ps_ref, buf_ref, y_ref, nbuf_ref, ext, *, nt, nb, start):
    j = pl.program_id(1)
    rows = nt * nb
    prev = POOL_BUF * nb

    @pl.when(j == 0)
    def _():
        ext[0:prev, :] = buf_ref[...].reshape(prev, BRANCH_W)

    x = x_ref[...].reshape(rows, D_MODEL)
    h = _rms_scale(x, g_ref[...]).astype(BF16)
    pxz = _dot(h, w_ref[...])
    px = pxz[:, :BRANCH_W]
    pz = pxz[:, BRANCH_W:]
    ext[prev:prev + rows, :] = px

    t_local = lax.broadcasted_iota(jnp.int32, (rows, POOL_GW), 0) // nb
    pos1 = start + j * nt + t_local + 1
    outs = []
    for gi, win in enumerate(POOL_WINDOWS):
        cs = slice(POOL_GW * gi, POOL_GW * (gi + 1))
        acc = ext[prev:prev + rows, cs]
        for q in range(1, win):
            acc = acc + ext[prev - q * nb:prev - q * nb + rows, cs]
        cnt = jnp.minimum(pos1, win).astype(F32)
        outs.append(acc / cnt)
    pooled = jnp.concatenate(outs, axis=1) - px
    mixed = _dot(pooled.astype(BF16), pw_ref[...]) * ps_ref[...]
    y_ref[...] = (mixed * _silu(pz)).astype(y_ref.dtype).reshape(y_ref.shape)
    last = ext[rows:rows + prev, :]
    nbuf_ref[...] = last.reshape(nbuf_ref.shape)
    ext[0:prev, :] = last


def _pool_call(x, wl, buf, *, l, sl, seq, batch, nt, nb, start, y_dtype):
    rows = nt * nb
    grid = (batch // nb, seq // nt)
    if nb == batch:
        x_spec = pl.BlockSpec((rows, D_MODEL), lambda i, j: (j, 0))
        y_spec = pl.BlockSpec((rows, BRANCH_W), lambda i, j: (j, 0))
        y_shape = jax.ShapeDtypeStruct((seq * batch, BRANCH_W), y_dtype)
        b_in = pl.BlockSpec((None, POOL_BUF * nb, BRANCH_W), lambda i, j: (sl, 0, 0))
        b_spec = pl.BlockSpec((POOL_BUF * nb, BRANCH_W), lambda i, j: (0, 0))
        b_shape = jax.ShapeDtypeStruct((POOL_BUF * batch, BRANCH_W), F32)
    else:
        x_spec = pl.BlockSpec((nt, nb, D_MODEL), lambda i, j: (j, i, 0))
        y_spec = pl.BlockSpec((nt, nb, BRANCH_W), lambda i, j: (j, i, 0))
        y_shape = jax.ShapeDtypeStruct((seq, batch, BRANCH_W), y_dtype)
        b_in = pl.BlockSpec((None, POOL_BUF, nb, BRANCH_W), lambda i, j: (sl, 0, i, 0))
        b_spec = pl.BlockSpec((POOL_BUF, nb, BRANCH_W), lambda i, j: (0, i, 0))
        b_shape = jax.ShapeDtypeStruct((POOL_BUF, batch, BRANCH_W), F32)
    weights, w_specs = zip(*[_weight_operand(wl[k], l) for k in ("g_pre", "w_pool", "pool_w", "pool_scale")])
    return pl.pallas_call(
        functools.partial(_pool_body, nt=nt, nb=nb, start=start),
        grid=grid,
        in_specs=[x_spec] + list(w_specs) + [b_in],
        out_specs=[y_spec, b_spec],
        out_shape=[y_shape, b_shape],
        scratch_shapes=[pltpu.VMEM(((POOL_BUF + nt) * nb, BRANCH_W), F32)],
        compiler_params=_params(("arbitrary", "arbitrary")),
    )(x, *weights, buf)


def _seg_sums(xs, ones):
    rows = xs[0].shape[0]
    stacked = jnp.concatenate([x.astype(BF16) for x in xs], axis=0)
    s = jnp.concatenate([_dot(stacked[:, MXU_TILE * g:MXU_TILE * (g + 1)], ones)
                         for g in range(BRANCH_W // MXU_TILE)], axis=1)
    return [s[i * rows:(i + 1) * rows] for i in range(len(xs))]


def _block_rows(y, head_masks):
    yb = y.astype(BF16)
    return jnp.concatenate([yb * m for m in head_masks], axis=0)


def _rw_body(x_ref, s0_ref, sh_ref, g_ref, w_ref, perm_ref, permt_ref, mu_ref, w0_ref, wl_ref, a0_ref,
             kkw_ref, kaw_ref, rkw_ref, gnw_ref, gnb_ref, ones_ref, tri_ref,
             y_ref, sout_ref, shout_ref,
             sbd, sbb, carry, qk_s, qr_s, kb_s, kk_s, v_s, ei_s, bonus_s, z_s, o_s,
             *, nt, nbb, part, gw):
    j = pl.program_id(1)
    rows = nt * nbb
    n_ch = rows // CHUNK
    spc = CHUNK // nt
    log_seq = nt.bit_length() - 1

    n_grp = BRANCH_W // gw
    grp_heads = gw // RWKV_HEAD

    lane = lax.broadcasted_iota(jnp.int32, (CHUNK, gw), 1)
    row64 = lax.broadcasted_iota(jnp.int32, (CHUNK, gw), 0)
    src = lane & (CHUNK - 1)
    same_seq = (row64 >> log_seq) == (src >> log_seq)
    m_strict = same_seq & (src < row64)
    m_incl = same_seq & (src <= row64)
    eye_sbs = (src == row64).astype(F32)
    head_masks = [((lane >> 6) == hh).astype(BF16) for hh in range(grp_heads)]
    half_lane = lax.broadcasted_iota(jnp.int32, (RWKV_HEAD, LANES), 1)
    half_masks = [half_lane < RWKV_HEAD, half_lane >= RWKV_HEAD]

    @pl.when(j == 0)
    def _():
        zero_blk = jnp.zeros((RWKV_HEAD, RWKV_HEAD), F32)
        for q in range(nbb):
            for g in range(n_grp):
                sbd[q, g] = jnp.zeros((gw, gw), F32)
                sbb[q, g] = jnp.zeros((gw, gw), BF16)
                for hh in range(grp_heads):
                    hr = slice(RWKV_HEAD * hh, RWKV_HEAD * (hh + 1))
                    hl = slice(LANES * (hh // 2), LANES * (hh // 2 + 1))
                    blk = s0_ref[q, gw * g + RWKV_HEAD * hh:gw * g + RWKV_HEAD * (hh + 1), :]
                    piece = jnp.concatenate([blk, zero_blk] if hh % 2 == 0 else [zero_blk, blk], axis=1)
                    sbd[q, g, hr, hl] = piece
                    sbb[q, g, hr, hl] = piece.astype(BF16)
        carry[...] = sh_ref[...]

    x = x_ref[...].reshape(rows, D_MODEL)
    h = _rms_scale(x, g_ref[...]).astype(BF16)
    hb = _dot(perm_ref[...], h).astype(BF16)
    ones = ones_ref[...]

    is_first = (lax.broadcasted_iota(jnp.int32, (part, RWKV_SHIFT_W), 0) & (nt - 1)) == 0
    lane128 = lax.broadcasted_iota(jnp.int32, (part, 2 * LORA_RANK), 1)

    def prepare(r0):
        rs = slice(r0, r0 + part)
        q0 = r0 // nt
        pz = _dot(hb[rs], w_ref[...])
        yield
        p = pz[:, :RWKV_SHIFT_W]
        z_s[rs, :] = pz[:, RWKV_SHIFT_W:]
        first = jnp.concatenate(
            [jnp.broadcast_to(carry[q0 + q:q0 + q + 1, :], (nt, RWKV_SHIFT_W)) for q in range(part // nt)], axis=0)
        prev = jnp.where(is_first, first, pltpu.roll(p, 1, 0))
        for q in range(part // nt):
            carry[q0 + q:q0 + q + 1, :] = p[nt * (q + 1) - 1:nt * (q + 1), :]
        pm = p + (prev - p) * mu_ref[...]
        yield
        r = pm[:, 0:BRANCH_W]
        k = pm[:, BRANCH_W:2 * BRANCH_W]
        v = pm[:, 2 * BRANCH_W:3 * BRANCH_W]
        ll = pm[:, 3 * BRANCH_W:]
        xl = jnp.where(lane128 < LORA_RANK, jnp.tanh(ll), ll).astype(BF16)
        dl = _dot(xl, wl_ref[...])
        yield
        w_log = -jax.nn.softplus(-(w0_ref[...] + dl[:, :BRANCH_W])) - 0.5
        ld = -jnp.exp(w_log)
        yield
        a = _sigmoid(a0_ref[...] + dl[:, BRANCH_W:])
        kk = k * kkw_ref[...]
        k2 = k * (1.0 + (a - 1.0) * kaw_ref[...])
        yield
        ssq, rk = _seg_sums([kk * kk, r * k2 * rkw_ref[...]], ones)
        hi, mid, lo = _split3(ld)
        cum3 = _dot(tri_ref[...], jnp.concatenate([hi, mid, lo], axis=1))
        yield
        kkn = kk * lax.rsqrt(jnp.maximum(ssq, 1e-24))
        cum = cum3[:, :BRANCH_W] + cum3[:, BRANCH_W:2 * BRANCH_W] + cum3[:, 2 * BRANCH_W:]
        e_incl = jnp.exp(cum)
        yield
        e_inv = jnp.exp(-cum)
        yield
        qk_s[rs, :] = (kkn * jnp.exp(cum - ld)).astype(BF16)
        qr_s[rs, :] = r * e_incl
        yield
        kb_s[rs, :] = (kkn * a * e_inv).astype(BF16)
        kk_s[rs, :] = (k2 * e_inv).astype(BF16)
        v_s[rs, :] = v.astype(BF16)
        ei_s[rs, :] = e_incl
        bonus_s[rs, :] = rk * v

    def chunk(c, g):
        r0 = c * CHUNK if isinstance(c, int) else pl.multiple_of(c * CHUNK, CHUNK)
        rs = pl.ds(r0, CHUNK)
        ls = slice(gw * g, gw * (g + 1))
        q_k = qk_s[rs, ls]
        q_r = qr_s[rs, ls]
        k_b = kb_s[rs, ls]
        k_k = kk_s[rs, ls]
        vb = v_s[rs, ls]
        q2 = jnp.concatenate([q_k, q_r.astype(BF16)], axis=0)
        bd_v = _block_rows(vb, head_masks)
        a_all = _dot_nt(q2, jnp.concatenate([_block_rows(k_b, head_masks), _block_rows(k_k, head_masks)], axis=0))
        yield
        a_kb = jnp.where(m_strict, a_all[:CHUNK, :gw], 0.0)
        a_rb = jnp.where(m_incl, a_all[CHUNK:, :gw], 0.0).astype(BF16)
        a_kk = jnp.where(m_strict, a_all[:CHUNK, gw:], 0.0).astype(BF16)
        a_rk = jnp.where(m_incl, a_all[CHUNK:, gw:], 0.0).astype(BF16)
        av = _dot(jnp.concatenate([a_kk, a_rk], axis=0), bd_v)
        npow = -a_kb
        tinv = eye_sbs + npow
        npow = _dot(npow.astype(BF16), _block_rows(npow, head_masks))
        yield
        akv = av[:CHUNK]
        o_v = av[CHUNK:]
        for it in range(1, log_seq):
            bd_n = _block_rows(npow, head_masks)
            if it + 1 < log_seq:
                both = _dot(jnp.concatenate([npow.astype(BF16), tinv.astype(BF16)], axis=0), bd_n)
                yield
                npow = both[:CHUNK]
                tinv = tinv + both[CHUNK:]
            else:
                last = _dot(tinv.astype(BF16), bd_n)
                yield
                tinv = tinv + last
        tw = _dot(tinv.astype(BF16),
                  jnp.concatenate([_block_rows(q_k, head_masks), _block_rows(-akv, head_masks)], axis=1))
        yield
        qt_k = tw[:, :gw]
        w_loc = tw[:, gw:]
        d1 = _dot(a_rb, jnp.concatenate([_block_rows(qt_k, head_masks), _block_rows(w_loc, head_masks)], axis=1))
        yield
        qh_r = q_r - d1[:, :gw]
        o_loc = d1[:, gw:] + o_v

        def update_state(sq, zz, pc):
            for hh in range(grp_heads):
                hr = slice(RWKV_HEAD * hh, RWKV_HEAD * (hh + 1))
                hl = slice(LANES * (hh // 2), LANES * (hh // 2 + 1))
                piece = (sbd[sq, g, hr, hl] + jnp.where(half_masks[hh % 2], zz[hr, hl], 0.0)) * pc[:, hl]
                sbd[sq, g, hr, hl] = piece
                sbb[sq, g, hr, hl] = piece.astype(BF16)

        if spc == 1:
            xs = _dot_nt(jnp.concatenate([qt_k.astype(BF16), qh_r.astype(BF16)], axis=0), sbb[c, g])
            yield
            u = w_loc - xs[:CHUNK]
            o_s[rs, ls] = xs[CHUNK:] + o_loc
            zz = _dot_tn(jnp.concatenate([u.astype(BF16), vb], axis=0), jnp.concatenate([k_b, k_k], axis=0))
            yield
            update_state(c, zz, ei_s[pl.ds(r0 + CHUNK - 1, 1), ls])
        else:
            kb_f = k_b.astype(F32)
            kk_f = k_k.astype(F32)
            v_f = vb.astype(F32)
            seqs = [slice(nt * q, nt * (q + 1)) for q in range(spc)]
            xs = [_dot_nt(jnp.concatenate([qt_k[qs], qh_r[qs]], axis=0).astype(BF16), sbb[c * spc + q, g])
                  for q, qs in enumerate(seqs)]
            yield
            o_s[rs, ls] = jnp.concatenate([x[nt:] for x in xs], axis=0) + o_loc
            zz = [_dot_tn(jnp.concatenate([w_loc[qs] - x[:nt], v_f[qs]], axis=0).astype(BF16),
                          jnp.concatenate([kb_f[qs], kk_f[qs]], axis=0).astype(BF16))
                  for x, qs in zip(xs, seqs)]
            yield
            for q in range(spc):
                update_state(c * spc + q, zz[q], ei_s[pl.ds(r0 + nt * (q + 1) - 1, 1), ls])

    def run_together(live):
        while live:
            still = []
            for gen in live:
                try:
                    next(gen)
                    still.append(gen)
                except StopIteration:
                    pass
            live = still

    run_together([prepare(r0) for r0 in range(0, rows, part)])
    if n_ch <= RW_UNROLL:
        run_together([chunk(c, g) for c in range(n_ch) for g in range(n_grp)])
    else:
        def loop_body(cc, carry_):
            run_together([chunk(cc * RW_UNROLL + i, g) for i in range(RW_UNROLL) for g in range(n_grp)])
            return carry_
        lax.fori_loop(0, n_ch // RW_UNROLL, loop_body, 0)

    ybs = []
    for r0 in range(0, rows, part):
        rs = slice(r0, r0 + part)
        o = o_s[rs, :]
        mean = _seg_sums([o], ones)[0] * (1.0 / RWKV_HEAD)
        dev = o - mean
        var = _seg_sums([dev * dev], ones)[0] * (1.0 / RWKV_HEAD)
        o = dev * lax.rsqrt(var + GN_EPS) * gnw_ref[...] + gnb_ref[...] + bonus_s[rs, :]
        ybs.append((o * _silu(z_s[rs, :])).astype(BF16))
    y_tm = _dot(permt_ref[...], jnp.concatenate(ybs, axis=0))
    y_ref[...] = y_tm.astype(y_ref.dtype).reshape(y_ref.shape)
    shout_ref[...] = carry[...]

    @pl.when(j == pl.num_programs(1) - 1)
    def _():
        for q in range(nbb):
            for g in range(n_grp):
                for hh in range(grp_heads):
                    hr = slice(RWKV_HEAD * hh, RWKV_HEAD * (hh + 1))
                    hl = slice(RWKV_HEAD * hh, RWKV_HEAD * (hh + 1))
                    sout_ref[q, gw * g + RWKV_HEAD * hh:gw * g + RWKV_HEAD * (hh + 1), :] = sbd[q, g, hr, hl]


def _rw_constants(nt, nbb, part):
    rows = nt * nbb
    r = np.arange(rows)
    src = (r % nt) * nbb + (r // nt)
    perm = np.zeros((rows, rows), np.float32)
    perm[r, src] = 1.0
    t = np.arange(part)
    tri = ((t[:, None] // nt) == (t[None, :] // nt)) & (t[None, :] <= t[:, None])
    return jnp.asarray(perm, BF16), jnp.asarray(perm.T, BF16), jnp.asarray(tri, BF16)


def _rw_call(x, shift0, s0, wl, *, l, sl, seq, batch, nt, nbb, y_dtype):
    rows = nt * nbb
    part = min(rows, 256)
    gw = RW_GROUP if nt == CHUNK else MXU_TILE
    grid = (batch // nbb, seq // nt)
    if nbb == batch:
        x_spec = pl.BlockSpec((rows, D_MODEL), lambda i, j: (j, 0))
        y_spec = pl.BlockSpec((rows, BRANCH_W), lambda i, j: (j, 0))
        y_shape = jax.ShapeDtypeStruct((seq * batch, BRANCH_W), y_dtype)
    else:
        x_spec = pl.BlockSpec((nt, nbb, D_MODEL), lambda i, j: (j, i, 0))
        y_spec = pl.BlockSpec((nt, nbb, BRANCH_W), lambda i, j: (j, i, 0))
        y_shape = jax.ShapeDtypeStruct((seq, batch, BRANCH_W), y_dtype)
    st_in = pl.BlockSpec((None, nbb, BRANCH_W, RWKV_HEAD), lambda i, j: (sl, i, 0, 0))
    sh_in = pl.BlockSpec((None, nbb, RWKV_SHIFT_W), lambda i, j: (sl, i, 0))
    st_spec = pl.BlockSpec((nbb, BRANCH_W, RWKV_HEAD), lambda i, j: (i, 0, 0))
    sh_spec = pl.BlockSpec((nbb, RWKV_SHIFT_W), lambda i, j: (i, 0))
    perm, permt, tri = _rw_constants(nt, nbb, part)
    lw = lambda name: _weight_operand(wl[name], l)
    const = lambda a: (a, _full_spec(a.shape))
    operands = (lw("g_pre"), lw("w_rw"), const(perm), const(permt), lw("rw_mu"), lw("rw_w0"), lw("rw_lora"),
                lw("rw_a0"), lw("rw_kk"), lw("rw_ka"), lw("rw_rk"), lw("rw_gnw"), lw("rw_gnb"),
                const(wl["rw_ones"]), const(tri))
    weights = [a for a, _ in operands]
    work = lambda dt: pltpu.VMEM((rows, BRANCH_W), dt)
    return pl.pallas_call(
        functools.partial(_rw_body, nt=nt, nbb=nbb, part=part, gw=gw),
        grid=grid,
        in_specs=[x_spec, st_in, sh_in] + [spec for _, spec in operands],
        out_specs=[y_spec, st_spec, sh_spec],
        out_shape=[y_shape, jax.ShapeDtypeStruct((batch, BRANCH_W, RWKV_HEAD), F32),
                   jax.ShapeDtypeStruct((batch, RWKV_SHIFT_W), F32)],
        scratch_shapes=[pltpu.VMEM((nbb, BRANCH_W // gw, gw, gw), F32),
                        pltpu.VMEM((nbb, BRANCH_W // gw, gw, gw), BF16),
                        pltpu.VMEM((nbb, RWKV_SHIFT_W), F32),
                        work(BF16), work(F32), work(BF16), work(BF16), work(BF16), work(F32), work(F32),
                        work(F32), work(F32)],
        compiler_params=_params(("arbitrary", "arbitrary")),
    )(x, s0, shift0, *weights)


def _merge_body(x_ref, ys5_ref, yrw_ref, ypl_ref, gpre_ref, wg_ref, wu_ref, wo_ref, gpost_ref, o_ref):
    x = x_ref[...]
    h = _rms_scale(x, gpre_ref[...]).astype(BF16)
    merged = None
    for i, y_ref in enumerate((ys5_ref, yrw_ref, ypl_ref)):
        gate = _sigmoid(_dot(h, wg_ref[:, D_MODEL * i:D_MODEL * (i + 1)]))
        term = gate * _dot(y_ref[...].astype(BF16), wu_ref[i])
        merged = term if merged is None else merged + term
    out = _dot(merged.astype(BF16), wo_ref[...])
    o_ref[...] = x + _rms_scale(out, gpost_ref[...])


def _merge_call(x, ys5, yrw, ypl, wl, *, l, tm):
    m = x.shape[0]
    tm = min(tm, m)
    weights, w_specs = zip(*[_weight_operand(wl[k], l) for k in ("g_pre", "w_gate", "w_up", "w_o", "g_post")])
    row_spec = lambda w: pl.BlockSpec((tm, w), lambda i: (i, 0))
    return pl.pallas_call(
        _merge_body,
        grid=(m // tm,),
        in_specs=[row_spec(D_MODEL), row_spec(BRANCH_W), row_spec(BRANCH_W), row_spec(BRANCH_W)]
        + list(w_specs),
        out_specs=row_spec(D_MODEL),
        out_shape=jax.ShapeDtypeStruct((m, D_MODEL), F32),
        compiler_params=_params(("parallel",)),
    )(x, ys5, yrw, ypl, *weights)


def _block_diag(blocks):
    *lead, n, r, c = blocks.shape
    eye = jnp.eye(n, dtype=blocks.dtype)
    return jnp.einsum("...grc,gh->...grhc", blocks, eye).reshape(*lead, n * r, n * c)


def _prep_weights(norm_pre, norm_post, w_in, s5_lam_re, s5_lam_im, s5_log_dt, s5_b_re, s5_b_im,
                  s5_c_re, s5_c_im, s5_d, s5_w_glu, rwkv_mu, rwkv_w0, rwkv_w_w2, rwkv_a0, rwkv_w_a2,
                  rwkv_k_k, rwkv_k_a, rwkv_r_k, rwkv_gn_w, rwkv_gn_b, pool_w, pool_scale,
                  w_up_s5, w_up_rwkv, w_up_pool, w_o):
    depth = w_in.shape[0]
    row = lambda a: a.astype(F32).reshape(depth, 1, -1)
    wl = {}
    wl["g_pre"] = row(norm_pre)
    wl["g_post"] = row(norm_post)
    wl["w_s5"] = w_in[:, :, 0:1024].astype(BF16)
    wl["w_rw"] = w_in[:, :, 1024:3200].astype(BF16)
    wl["w_pool"] = w_in[:, :, 3200:4224].astype(BF16)
    wl["w_gate"] = w_in[:, :, 4224:7296].astype(BF16)

    lr = s5_lam_re.astype(F32)
    li = s5_lam_im.astype(F32)
    dt = jnp.exp(s5_log_dt.astype(F32))[:, :, None]
    mag = jnp.exp(lr * dt)
    ab_r = mag * jnp.cos(li * dt)
    ab_i = mag * jnp.sin(li * dt)
    den = lr * lr + li * li
    nr = ab_r - 1.0
    co_r = ((nr * lr + ab_i * li) / den)[..., None]
    co_i = ((ab_i * lr - nr * li) / den)[..., None]
    b_re = s5_b_re.astype(F32)
    b_im = s5_b_im.astype(F32)
    bf_re = co_r * b_re - co_i * b_im
    bf_im = co_r * b_im + co_i * b_re
    gh = S5_GROUPS // 2
    wb = []
    for k in range(2):
        gs = slice(gh * k, gh * (k + 1))
        wb.append(jnp.concatenate([_block_diag(jnp.swapaxes(bf_re[:, gs], 2, 3)),
                                   _block_diag(jnp.swapaxes(bf_im[:, gs], 2, 3))], axis=2))
    wl["s5_wb"] = jnp.stack(wb, axis=1).astype(BF16)
    wl["s5_ar"] = ab_r.reshape(depth, 1, S5_NSTATE)
    wl["s5_ai"] = ab_i.reshape(depth, 1, S5_NSTATE)
    c_re = s5_c_re.astype(F32)
    c_im = s5_c_im.astype(F32)
    cc = []
    for k in range(2):
        gs = slice(gh * k, gh * (k + 1))
        cc.append(jnp.concatenate([_block_diag(jnp.swapaxes(c_re[:, gs], 2, 3)),
                                   _block_diag(jnp.swapaxes(-c_im[:, gs], 2, 3))], axis=1))
    wl["s5_cc"] = jnp.stack(cc, axis=1).astype(BF16)
    wl["s5_d"] = row(s5_d)
    wl["s5_wglu"] = s5_w_glu.astype(BF16)

    wl["rw_mu"] = row(rwkv_mu)
    wl["rw_w0"] = row(rwkv_w0)
    zeros = jnp.zeros((depth, LORA_RANK, BRANCH_W), F32)
    wl["rw_lora"] = jnp.concatenate(
        [jnp.concatenate([rwkv_w_w2.astype(F32), zeros], axis=2),
         jnp.concatenate([zeros, rwkv_w_a2.astype(F32)], axis=2)], axis=1).astype(BF16)
    wl["rw_a0"] = row(rwkv_a0)
    wl["rw_kk"] = row(rwkv_k_k)
    wl["rw_ka"] = row(rwkv_k_a)
    wl["rw_rk"] = row(rwkv_r_k)
    wl["rw_gnw"] = row(rwkv_gn_w)
    wl["rw_gnb"] = row(rwkv_gn_b)
    wl["rw_ones"] = _block_diag(jnp.ones((MXU_TILE // RWKV_HEAD, RWKV_HEAD, RWKV_HEAD), F32)).astype(BF16)

    wl["pool_w"] = _block_diag(pool_w.astype(F32)).astype(BF16)
    wl["pool_scale"] = row(pool_scale)
    wl["w_up"] = jnp.stack([w_up_s5, w_up_rwkv, w_up_pool], axis=1).astype(BF16)
    wl["w_o"] = w_o.astype(BF16)
    return wl


def _layer(x, states, wl, *, l, sl, seq, batch, start, prompt):
    s5r, s5i, rws, rwsh, pbuf = states
    if prompt:
        nt, nb, nbb, y_dtype = CHUNK, batch, batch, BF16
        nts = min(seq, 2 * CHUNK)
    else:
        nt, nb, nbb, y_dtype = seq, min(batch, 32), min(batch, 16), F32
        nts = seq
    x3 = x.reshape(seq, batch, D_MODEL)
    flat = lambda y: y.reshape(seq * batch, BRANCH_W)
    if nb == batch:
        x_tm = x
        pbuf_in = pbuf.reshape(pbuf.shape[0], POOL_BUF * batch, BRANCH_W)
    else:
        x_tm = x3
        pbuf_in = pbuf
    ys5, n_re, n_im = _s5_call(x_tm, wl, s5r, s5i, l=l, sl=sl, seq=seq, batch=batch, nt=nts, nb=nb, y_dtype=y_dtype)
    ypl, n_buf = _pool_call(x_tm, wl, pbuf_in, l=l, sl=sl, seq=seq, batch=batch, nt=nts, nb=nb, start=start,
                            y_dtype=y_dtype)
    yrw, n_s, n_shift = _rw_call(x if nbb == batch else x3, rwsh, rws, wl, l=l, sl=sl, seq=seq, batch=batch,
                                 nt=nt, nbb=nbb, y_dtype=y_dtype)
    x_new = _merge_call(x, flat(ys5), flat(yrw), flat(ypl), wl, l=l, tm=1024)
    n_buf = n_buf.reshape(POOL_BUF, batch, BRANCH_W)
    return x_new, (n_re, n_im, n_s, n_shift, n_buf)


def kernel(x_prompt, x_sample, state_s5_re, state_s5_im, state_rwkv, state_shift, state_pool, norm_pre, norm_post, w_in, s5_lam_re, s5_lam_im, s5_log_dt, s5_b_re, s5_b_im, s5_c_re, s5_c_im, s5_d, s5_w_glu, rwkv_mu, rwkv_w0, rwkv_w_w2, rwkv_a0, rwkv_w_a2, rwkv_k_k, rwkv_k_a, rwkv_r_k, rwkv_gn_w, rwkv_gn_b, pool_w, pool_scale, w_up_s5, w_up_rwkv, w_up_pool, w_o):
    depth = w_in.shape[0]
    bp, tp, _ = x_prompt.shape
    bs, ts, _ = x_sample.shape
    xp = jnp.transpose(x_prompt, (1, 0, 2)).reshape(tp * bp, D_MODEL)
    xs = jnp.transpose(x_sample, (1, 0, 2)).reshape(ts * bs, D_MODEL)
    outs_p = [[] for _ in range(5)]
    outs_s = [[] for _ in range(5)]
    wl = _prep_weights(norm_pre, norm_post, w_in, s5_lam_re, s5_lam_im, s5_log_dt, s5_b_re, s5_b_im,
                       s5_c_re, s5_c_im, s5_d, s5_w_glu, rwkv_mu, rwkv_w0, rwkv_w_w2, rwkv_a0, rwkv_w_a2,
                       rwkv_k_k, rwkv_k_a, rwkv_r_k, rwkv_gn_w, rwkv_gn_b, pool_w, pool_scale,
                       w_up_s5, w_up_rwkv, w_up_pool, w_o)
    zeros_p = (jnp.zeros((1, bp, S5_NSTATE), F32), jnp.zeros((1, bp, S5_NSTATE), F32),
               jnp.zeros((1, bp, BRANCH_W, RWKV_HEAD), F32), jnp.zeros((1, bp, RWKV_SHIFT_W), F32),
               jnp.zeros((1, POOL_BUF, bp, BRANCH_W), F32))
    for l in range(depth):
        xp, new_p = _layer(xp, zeros_p, wl, l=l, sl=0, seq=tp, batch=bp, start=0, prompt=True)
        st_s = (state_s5_re[l].reshape(1, bs, S5_NSTATE), state_s5_im[l].reshape(1, bs, S5_NSTATE),
                state_rwkv[l].reshape(1, bs, BRANCH_W, RWKV_HEAD), state_shift[l][None],
                jnp.transpose(state_pool[l], (1, 0, 2))[None])
        xs, new_s = _layer(xs, st_s, wl, l=l, sl=0, seq=ts, batch=bs, start=PAST_LEN, prompt=False)
        for lst, val in zip(outs_p, new_p):
            lst.append(val)
        for lst, val in zip(outs_s, new_s):
            lst.append(val)

    def finish(outs, b):
        re, im, rw, sh, pb = [jnp.stack(v) for v in outs]
        return (re.reshape(depth, b, S5_GROUPS, S5_STATE), im.reshape(depth, b, S5_GROUPS, S5_STATE),
                rw.reshape(depth, b, RWKV_HEADS, RWKV_HEAD, RWKV_HEAD), sh,
                jnp.transpose(pb, (0, 2, 1, 3)))

    y_p = jnp.transpose(xp.reshape(tp, bp, D_MODEL), (1, 0, 2))
    y_s = jnp.transpose(xs.reshape(ts, bs, D_MODEL), (1, 0, 2))
    return (y_p, y_s) + finish(outs_p, bp) + finish(outs_s, bs)
```

```python
import functools
import math
from typing import NamedTuple

import jax
import jax.numpy as jnp
import numpy as np
from jax import lax
from jax.experimental import pallas as pl
from jax.experimental.pallas import tpu as pltpu

F32 = jnp.float32
BF16 = jnp.bfloat16

D_MODEL = 1024
BRANCH_W = 512
S5_GROUPS = 32
S5_GROUP = 16
S5_STATE = 64
S5_NSTATE = S5_GROUPS * S5_STATE
RWKV_HEADS = 8
RWKV_HEAD = 64
LORA_RANK = 64
RWKV_SHIFT_W = 3 * BRANCH_W + 2 * LORA_RANK
POOL_WINDOWS = (2, 4, 8, 16)
POOL_GW = 128
POOL_BUF = 15
NORM_EPS = 1e-6
GN_EPS = 64e-5
PAST_LEN = 16384

MXU_TILE = 256
LANES = 128
CHUNK = 64
RW_GROUP = 256
RW_UNROLL = 8
VMEM_LIMIT = 56 * 1024 * 1024


def _dot(a, b):
    return jnp.dot(a, b, preferred_element_type=F32)


def _dot_nt(a, b):
    return lax.dot_general(a, b, (((1,), (1,)), ((), ())), preferred_element_type=F32)


def _dot_tn(a, b):
    return lax.dot_general(a, b, (((0,), (0,)), ((), ())), preferred_element_type=F32)


def _split3(x):
    hi = x.astype(BF16)
    r1 = x - hi.astype(F32)
    mid = r1.astype(BF16)
    lo = (r1 - mid.astype(F32)).astype(BF16)
    return hi, mid, lo


def _rms_scale(x, g):
    ms = jnp.mean(x * x, axis=-1, keepdims=True)
    return x * lax.rsqrt(ms + NORM_EPS) * g


def _sigmoid(x):
    return jax.nn.sigmoid(x)


def _silu(x):
    return x * jax.nn.sigmoid(x)


def _gelu_tanh(x):
    c = math.sqrt(2.0 / math.pi)
    return 0.5 * x * (1.0 + jnp.tanh(c * (x + 0.044715 * (x * x * x))))


def _full_spec(shape):
    nd = len(shape)
    return pl.BlockSpec(shape, lambda *_: (0,) * nd)


def _layer_spec(shape, l):
    nd = len(shape) - 1
    return pl.BlockSpec((None,) + tuple(shape[1:]), lambda *_: (l,) + (0,) * nd)


class _ColWindow(NamedTuple):
    array: jax.Array
    block_cols: int
    block_index: int
    offset: int


def _weight_operand(w, l):
    if isinstance(w, _ColWindow):
        rows = w.array.shape[1]
        return w.array, pl.BlockSpec((None, rows, w.block_cols), lambda *_: (l, 0, w.block_index))
    return w, _layer_spec(w.shape, l)


def _params(sem):
    return pltpu.CompilerParams(dimension_semantics=sem, vmem_limit_bytes=VMEM_LIMIT)


def _s5_body(x_ref, g_ref, w_ref, wb_ref, ar_ref, ai_ref, cc_ref, d_ref, wglu_ref, h0r_ref, h0i_ref,
             y_ref, nr_ref, ni_ref, bbr, bbi, cr, ci, *, nt, nb):
    j = pl.program_id(1)
    rows = nt * nb
    half = S5_NSTATE // 2

    @pl.when(j == 0)
    def _():
        cr[...] = h0r_ref[...]
        ci[...] = h0i_ref[...]

    x = x_ref[...].reshape(rows, D_MODEL)
    h = _rms_scale(x, g_ref[...]).astype(BF16)
    sxz = _dot(h, w_ref[...])
    sx = sxz[:, :BRANCH_W]
    sz = sxz[:, BRANCH_W:]
    ub = sx.astype(BF16)
    for k in range(2):
        o = _dot(ub[:, MXU_TILE * k:MXU_TILE * (k + 1)], wb_ref[k])
        bbr[:, half * k:half * (k + 1)] = o[:, :half]
        bbi[:, half * k:half * (k + 1)] = o[:, half:]

    cw = max(128, min(1024, (8 * 1024) // nb))
    for c0 in range(0, S5_NSTATE, cw):
        cs = slice(c0, c0 + cw)
        a_r = jnp.broadcast_to(ar_ref[:, cs], (nb, cw))
        a_i = jnp.broadcast_to(ai_ref[:, cs], (nb, cw))

        def step(t, carry, cs=cs, a_r=a_r, a_i=a_i):
            hr, hi = carry
            rs = pl.ds(pl.multiple_of(t * nb, nb), nb)
            nhr = a_r * hr - a_i * hi + bbr[rs, cs]
            nhi = a_r * hi + a_i * hr + bbi[rs, cs]
            bbr[rs, cs] = nhr
            bbi[rs, cs] = nhi
            return nhr, nhi

        hr, hi = lax.fori_loop(0, nt, step, (cr[:, cs], ci[:, cs]), unroll=min(nt, 8))
        cr[:, cs] = hr
        ci[:, cs] = hi

    ys = []
    for k in range(2):
        ks = slice(half * k, half * (k + 1))
        lhs = jnp.concatenate([bbr[:, ks].astype(BF16), bbi[:, ks].astype(BF16)], axis=1)
        ys.append(_dot(lhs, cc_ref[k]))
    y = jnp.concatenate(ys, axis=1) + d_ref[...] * sx
    y = _gelu_tanh(y)
    y = y * _sigmoid(_dot(y.astype(BF16), wglu_ref[...]))
    y_ref[...] = (y * _silu(sz)).astype(y_ref.dtype).reshape(y_ref.shape)
    nr_ref[...] = cr[...]
    ni_ref[...] = ci[...]


def _s5_call(x, wl, h0r, h0i, *, l, sl, seq, batch, nt, nb, y_dtype):
    rows = nt * nb
    grid = (batch // nb, seq // nt)
    if nb == batch:
        x_spec = pl.BlockSpec((rows, D_MODEL), lambda i, j: (j, 0))
        y_spec = pl.BlockSpec((rows, BRANCH_W), lambda i, j: (j, 0))
        y_shape = jax.ShapeDtypeStruct((seq * batch, BRANCH_W), y_dtype)
    else:
        x_spec = pl.BlockSpec((nt, nb, D_MODEL), lambda i, j: (j, i, 0))
        y_spec = pl.BlockSpec((nt, nb, BRANCH_W), lambda i, j: (j, i, 0))
        y_shape = jax.ShapeDtypeStruct((seq, batch, BRANCH_W), y_dtype)
    st_in = pl.BlockSpec((None, nb, S5_NSTATE), lambda i, j: (sl, i, 0))
    st_spec = pl.BlockSpec((nb, S5_NSTATE), lambda i, j: (i, 0))
    st_shape = jax.ShapeDtypeStruct((batch, S5_NSTATE), F32)
    weights, w_specs = zip(*[_weight_operand(wl[k], l) for k in (
        "g_pre", "w_s5", "s5_wb", "s5_ar", "s5_ai", "s5_cc", "s5_d", "s5_wglu")])
    return pl.pallas_call(
        functools.partial(_s5_body, nt=nt, nb=nb),
        grid=grid,
        in_specs=[x_spec] + list(w_specs) + [st_in, st_in],
        out_specs=[y_spec, st_spec, st_spec],
        out_shape=[y_shape, st_shape, st_shape],
        scratch_shapes=[pltpu.VMEM((rows, S5_NSTATE), F32), pltpu.VMEM((rows, S5_NSTATE), F32),
                        pltpu.VMEM((nb, S5_NSTATE), F32), pltpu.VMEM((nb, S5_NSTATE), F32)],
        compiler_params=_params(("arbitrary", "arbitrary")),
    )(x, *weights, h0r, h0i)


def _pool_body(x_ref, g_ref, w_ref, pw_ref, ps_ref, buf_ref, y_ref, nbuf_ref, ext, *, nt, nb, start, w_off):
    j = pl.program_id(1)
    rows = nt * nb
    prev = POOL_BUF * nb

    @pl.when(j == 0)
    def _():
        ext[0:prev, :] = buf_ref[...].reshape(prev, BRANCH_W)

    x = x_ref[...].reshape(rows, D_MODEL)
    h = _rms_scale(x, g_ref[...]).astype(BF16)
    pxz = _dot(h, w_ref[:, w_off:])
    px = pxz[:, :BRANCH_W]
    pz = pxz[:, BRANCH_W:]
    ext[prev:prev + rows, :] = px

    t_local = lax.broadcasted_iota(jnp.int32, (rows, POOL_GW), 0) // nb
    pos1 = start + j * nt + t_local + 1
    outs = []
    for gi, win in enumerate(POOL_WINDOWS):
        cs = slice(POOL_GW * gi, POOL_GW * (gi + 1))
        acc = ext[prev:prev + rows, cs]
        for q in range(1, win):
            acc = acc + ext[prev - q * nb:prev - q * nb + rows, cs]
        cnt = jnp.minimum(pos1, win).astype(F32)
        outs.append(acc / cnt)
    pooled = jnp.concatenate(outs, axis=1) - px
    mixed = _dot(pooled.astype(BF16), pw_ref[...]) * ps_ref[...]
    y_ref[...] = (mixed * _silu(pz)).astype(y_ref.dtype).reshape(y_ref.shape)
    last = ext[rows:rows + prev, :]
    nbuf_ref[...] = last.reshape(nbuf_ref.shape)
    ext[0:prev, :] = last


def _pool_call(x, wl, buf, *, l, sl, seq, batch, nt, nb, start, y_dtype):
    rows = nt * nb
    grid = (batch // nb, seq // nt)
    if nb == batch:
        x_spec = pl.BlockSpec((rows, D_MODEL), lambda i, j: (j, 0))
        y_spec = pl.BlockSpec((rows, BRANCH_W), lambda i, j: (j, 0))
        y_shape = jax.ShapeDtypeStruct((seq * batch, BRANCH_W), y_dtype)
        b_in = pl.BlockSpec((None, POOL_BUF * nb, BRANCH_W), lambda i, j: (sl, 0, 0))
        b_spec = pl.BlockSpec((POOL_BUF * nb, BRANCH_W), lambda i, j: (0, 0))
        b_shape = jax.ShapeDtypeStruct((POOL_BUF * batch, BRANCH_W), F32)
    else:
        x_spec = pl.BlockSpec((nt, nb, D_MODEL), lambda i, j: (j, i, 0))
        y_spec = pl.BlockSpec((nt, nb, BRANCH_W), lambda i, j: (j, i, 0))
        y_shape = jax.ShapeDtypeStruct((seq, batch, BRANCH_W), y_dtype)
        b_in = pl.BlockSpec((None, POOL_BUF, nb, BRANCH_W), lambda i, j: (sl, 0, i, 0))
        b_spec = pl.BlockSpec((POOL_BUF, nb, BRANCH_W), lambda i, j: (0, i, 0))
        b_shape = jax.ShapeDtypeStruct((POOL_BUF, batch, BRANCH_W), F32)
    weights, w_specs = zip(*[_weight_operand(wl[k], l) for k in ("g_pre", "w_pool", "pool_w", "pool_scale")])
    return pl.pallas_call(
        functools.partial(_pool_body, nt=nt, nb=nb, start=start, w_off=wl["w_pool"].offset),
        grid=grid,
        in_specs=[x_spec] + list(w_specs) + [b_in],
        out_specs=[y_spec, b_spec],
        out_shape=[y_shape, b_shape],
        scratch_shapes=[pltpu.VMEM(((POOL_BUF + nt) * nb, BRANCH_W), F32)],
        compiler_params=_params(("arbitrary", "arbitrary")),
    )(x, *weights, buf)


def _seg_sums(xs, ones):
    rows = xs[0].shape[0]
    stacked = jnp.concatenate([x.astype(BF16) for x in xs], axis=0)
    s = jnp.concatenate([_dot(stacked[:, MXU_TILE * g:MXU_TILE * (g + 1)], ones)
                         for g in range(BRANCH_W // MXU_TILE)], axis=1)
    return [s[i * rows:(i + 1) * rows] for i in range(len(xs))]


def _block_rows(y, head_masks):
    yb = y.astype(BF16)
    return jnp.concatenate([yb * m for m in head_masks], axis=0)


def _rw_body(x_ref, s0_ref, sh_ref, g_ref, w_ref, perm_ref, permt_ref, mu_ref, w0_ref, wl_ref, a0_ref,
             kkw_ref, kaw_ref, rkw_ref, gnw_ref, gnb_ref, ones_ref, tri_ref,
             y_ref, sout_ref, shout_ref,
             sbd, sbb, carry, qk_s, qr_s, kb_s, kk_s, v_s, ei_s, bonus_s, z_s, o_s,
             *, nt, nbb, part, gw, w_off):
    j = pl.program_id(1)
    rows = nt * nbb
    n_ch = rows // CHUNK
    spc = CHUNK // nt
    log_seq = nt.bit_length() - 1

    n_grp = BRANCH_W // gw
    grp_heads = gw // RWKV_HEAD

    lane = lax.broadcasted_iota(jnp.int32, (CHUNK, gw), 1)
    row64 = lax.broadcasted_iota(jnp.int32, (CHUNK, gw), 0)
    src = lane & (CHUNK - 1)
    same_seq = (row64 >> log_seq) == (src >> log_seq)
    m_strict = same_seq & (src < row64)
    m_incl = same_seq & (src <= row64)
    eye_sbs = (src == row64).astype(F32)
    head_masks = [((lane >> 6) == hh).astype(BF16) for hh in range(grp_heads)]
    half_lane = lax.broadcasted_iota(jnp.int32, (RWKV_HEAD, LANES), 1)
    half_masks = [half_lane < RWKV_HEAD, half_lane >= RWKV_HEAD]

    @pl.when(j == 0)
    def _():
        zero_blk = jnp.zeros((RWKV_HEAD, RWKV_HEAD), F32)
        for q in range(nbb):
            for g in range(n_grp):
                sbd[q, g] = jnp.zeros((gw, gw), F32)
                sbb[q, g] = jnp.zeros((gw, gw), BF16)
                for hh in range(grp_heads):
                    hr = slice(RWKV_HEAD * hh, RWKV_HEAD * (hh + 1))
                    hl = slice(LANES * (hh // 2), LANES * (hh // 2 + 1))
                    blk = s0_ref[q, gw * g + RWKV_HEAD * hh:gw * g + RWKV_HEAD * (hh + 1), :]
                    piece = jnp.concatenate([blk, zero_blk] if hh % 2 == 0 else [zero_blk, blk], axis=1)
                    sbd[q, g, hr, hl] = piece
                    sbb[q, g, hr, hl] = piece.astype(BF16)
        carry[...] = sh_ref[...]

    x = x_ref[...].reshape(rows, D_MODEL)
    h = _rms_scale(x, g_ref[...]).astype(BF16)
    hb = _dot(perm_ref[...], h).astype(BF16)
    ones = ones_ref[...]

    is_first = (lax.broadcasted_iota(jnp.int32, (part, RWKV_SHIFT_W), 0) & (nt - 1)) == 0
    lane128 = lax.broadcasted_iota(jnp.int32, (part, 2 * LORA_RANK), 1)

    def prepare(r0):
        rs = slice(r0, r0 + part)
        q0 = r0 // nt
        pz = _dot(hb[rs], w_ref[:, w_off:])
        yield
        p = pz[:, :RWKV_SHIFT_W]
        z_s[rs, :] = pz[:, RWKV_SHIFT_W:]
        first = jnp.concatenate(
            [jnp.broadcast_to(carry[q0 + q:q0 + q + 1, :], (nt, RWKV_SHIFT_W)) for q in range(part // nt)], axis=0)
        prev = jnp.where(is_first, first, pltpu.roll(p, 1, 0))
        for q in range(part // nt):
            carry[q0 + q:q0 + q + 1, :] = p[nt * (q + 1) - 1:nt * (q + 1), :]
        pm = p + (prev - p) * mu_ref[...]
        yield
        r = pm[:, 0:BRANCH_W]
        k = pm[:, BRANCH_W:2 * BRANCH_W]
        v = pm[:, 2 * BRANCH_W:3 * BRANCH_W]
        ll = pm[:, 3 * BRANCH_W:]
        xl = jnp.where(lane128 < LORA_RANK, jnp.tanh(ll), ll).astype(BF16)
        dl = _dot(xl, wl_ref[...])
        yield
        w_log = -jax.nn.softplus(-(w0_ref[...] + dl[:, :BRANCH_W])) - 0.5
        ld = -jnp.exp(w_log)
        yield
        a = _sigmoid(a0_ref[...] + dl[:, BRANCH_W:])
        kk = k * kkw_ref[...]
        k2 = k * (1.0 + (a - 1.0) * kaw_ref[...])
        yield
        ssq, rk = _seg_sums([kk * kk, r * k2 * rkw_ref[...]], ones)
        hi, mid, lo = _split3(ld)
        cum3 = _dot(tri_ref[...], jnp.concatenate([hi, mid, lo], axis=1))
        yield
        kkn = kk * lax.rsqrt(jnp.maximum(ssq, 1e-24))
        cum = cum3[:, :BRANCH_W] + cum3[:, BRANCH_W:2 * BRANCH_W] + cum3[:, 2 * BRANCH_W:]
        e_incl = jnp.exp(cum)
        yield
        e_inv = jnp.exp(-cum)
        yield
        qk_s[rs, :] = (kkn * jnp.exp(cum - ld)).astype(BF16)
        qr_s[rs, :] = r * e_incl
        yield
        kb_s[rs, :] = (kkn * a * e_inv).astype(BF16)
        kk_s[rs, :] = (k2 * e_inv).astype(BF16)
        v_s[rs, :] = v.astype(BF16)
        ei_s[rs, :] = e_incl
        bonus_s[rs, :] = rk * v

    def chunk(c, g):
        r0 = c * CHUNK if isinstance(c, int) else pl.multiple_of(c * CHUNK, CHUNK)
        rs = pl.ds(r0, CHUNK)
        ls = slice(gw * g, gw * (g + 1))
        q_k = qk_s[rs, ls]
        q_r = qr_s[rs, ls]
        k_b = kb_s[rs, ls]
        k_k = kk_s[rs, ls]
        vb = v_s[rs, ls]
        q2 = jnp.concatenate([q_k, q_r.astype(BF16)], axis=0)
        bd_v = _block_rows(vb, head_masks)
        a_all = _dot_nt(q2, jnp.concatenate([_block_rows(k_b, head_masks), _block_rows(k_k, head_masks)], axis=0))
        yield
        a_kb = jnp.where(m_strict, a_all[:CHUNK, :gw], 0.0)
        a_rb = jnp.where(m_incl, a_all[CHUNK:, :gw], 0.0).astype(BF16)
        a_kk = jnp.where(m_strict, a_all[:CHUNK, gw:], 0.0).astype(BF16)
        a_rk = jnp.where(m_incl, a_all[CHUNK:, gw:], 0.0).astype(BF16)
        av = _dot(jnp.concatenate([a_kk, a_rk], axis=0), bd_v)
        npow = -a_kb
        tinv = eye_sbs + npow
        npow = _dot(npow.astype(BF16), _block_rows(npow, head_masks))
        yield
        akv = av[:CHUNK]
        o_v = av[CHUNK:]
        for it in range(1, log_seq):
            bd_n = _block_rows(npow, head_masks)
            if it + 1 < log_seq:
                both = _dot(jnp.concatenate([npow.astype(BF16), tinv.astype(BF16)], axis=0), bd_n)
                yield
                npow = both[:CHUNK]
                tinv = tinv + both[CHUNK:]
            else:
                last = _dot(tinv.astype(BF16), bd_n)
                yield
                tinv = tinv + last
        tw = _dot(tinv.astype(BF16),
                  jnp.concatenate([_block_rows(q_k, head_masks), _block_rows(-akv, head_masks)], axis=1))
        yield
        qt_k = tw[:, :gw]
        w_loc = tw[:, gw:]
        d1 = _dot(a_rb, jnp.concatenate([_block_rows(qt_k, head_masks), _block_rows(w_loc, head_masks)], axis=1))
        yield
        qh_r = q_r - d1[:, :gw]
        o_loc = d1[:, gw:] + o_v

        def update_state(sq, zz, pc):
            for hh in range(grp_heads):
                hr = slice(RWKV_HEAD * hh, RWKV_HEAD * (hh + 1))
                hl = slice(LANES * (hh // 2), LANES * (hh // 2 + 1))
                piece = (sbd[sq, g, hr, hl] + jnp.where(half_masks[hh % 2], zz[hr, hl], 0.0)) * pc[:, hl]
                sbd[sq, g, hr, hl] = piece
                sbb[sq, g, hr, hl] = piece.astype(BF16)

        if spc == 1:
            xs = _dot_nt(jnp.concatenate([qt_k.astype(BF16), qh_r.astype(BF16)], axis=0), sbb[c, g])
            yield
            u = w_loc - xs[:CHUNK]
            o_s[rs, ls] = xs[CHUNK:] + o_loc
            zz = _dot_tn(jnp.concatenate([u.astype(BF16), vb], axis=0), jnp.concatenate([k_b, k_k], axis=0))
            yield
            update_state(c, zz, ei_s[pl.ds(r0 + CHUNK - 1, 1), ls])
        else:
            kb_f = k_b.astype(F32)
            kk_f = k_k.astype(F32)
            v_f = vb.astype(F32)
            seqs = [slice(nt * q, nt * (q + 1)) for q in range(spc)]
            xs = [_dot_nt(jnp.concatenate([qt_k[qs], qh_r[qs]], axis=0).astype(BF16), sbb[c * spc + q, g])
                  for q, qs in enumerate(seqs)]
            yield
            o_s[rs, ls] = jnp.concatenate([x[nt:] for x in xs], axis=0) + o_loc
            zz = [_dot_tn(jnp.concatenate([w_loc[qs] - x[:nt], v_f[qs]], axis=0).astype(BF16),
                          jnp.concatenate([kb_f[qs], kk_f[qs]], axis=0).astype(BF16))
                  for x, qs in zip(xs, seqs)]
            yield
            for q in range(spc):
                update_state(c * spc + q, zz[q], ei_s[pl.ds(r0 + nt * (q + 1) - 1, 1), ls])

    def run_together(live):
        while live:
            still = []
            for gen in live:
                try:
                    next(gen)
                    still.append(gen)
                except StopIteration:
                    pass
            live = still

    run_together([prepare(r0) for r0 in range(0, rows, part)])
    if n_ch <= RW_UNROLL:
        run_together([chunk(c, g) for c in range(n_ch) for g in range(n_grp)])
    else:
        def loop_body(cc, carry_):
            run_together([chunk(cc * RW_UNROLL + i, g) for i in range(RW_UNROLL) for g in range(n_grp)])
            return carry_
        lax.fori_loop(0, n_ch // RW_UNROLL, loop_body, 0)

    ybs = []
    for r0 in range(0, rows, part):
        rs = slice(r0, r0 + part)
        o = o_s[rs, :]
        mean = _seg_sums([o], ones)[0] * (1.0 / RWKV_HEAD)
        dev = o - mean
        var = _seg_sums([dev * dev], ones)[0] * (1.0 / RWKV_HEAD)
        o = dev * lax.rsqrt(var + GN_EPS) * gnw_ref[...] + gnb_ref[...] + bonus_s[rs, :]
        ybs.append((o * _silu(z_s[rs, :])).astype(BF16))
    y_tm = _dot(permt_ref[...], jnp.concatenate(ybs, axis=0))
    y_ref[...] = y_tm.astype(y_ref.dtype).reshape(y_ref.shape)
    shout_ref[...] = carry[...]

    @pl.when(j == pl.num_programs(1) - 1)
    def _():
        for q in range(nbb):
            for g in range(n_grp):
                for hh in range(grp_heads):
                    hr = slice(RWKV_HEAD * hh, RWKV_HEAD * (hh + 1))
                    hl = slice(RWKV_HEAD * hh, RWKV_HEAD * (hh + 1))
                    sout_ref[q, gw * g + RWKV_HEAD * hh:gw * g + RWKV_HEAD * (hh + 1), :] = sbd[q, g, hr, hl]


def _rw_constants(nt, nbb, part):
    rows = nt * nbb
    r = np.arange(rows)
    src = (r % nt) * nbb + (r // nt)
    perm = np.zeros((rows, rows), np.float32)
    perm[r, src] = 1.0
    t = np.arange(part)
    tri = ((t[:, None] // nt) == (t[None, :] // nt)) & (t[None, :] <= t[:, None])
    return jnp.asarray(perm, BF16), jnp.asarray(perm.T, BF16), jnp.asarray(tri, BF16)


def _rw_call(x, shift0, s0, wl, *, l, sl, seq, batch, nt, nbb, y_dtype):
    rows = nt * nbb
    part = min(rows, 256)
    gw = RW_GROUP if nt == CHUNK else MXU_TILE
    grid = (batch // nbb, seq // nt)
    if nbb == batch:
        x_spec = pl.BlockSpec((rows, D_MODEL), lambda i, j: (j, 0))
        y_spec = pl.BlockSpec((rows, BRANCH_W), lambda i, j: (j, 0))
        y_shape = jax.ShapeDtypeStruct((seq * batch, BRANCH_W), y_dtype)
    else:
        x_spec = pl.BlockSpec((nt, nbb, D_MODEL), lambda i, j: (j, i, 0))
        y_spec = pl.BlockSpec((nt, nbb, BRANCH_W), lambda i, j: (j, i, 0))
        y_shape = jax.ShapeDtypeStruct((seq, batch, BRANCH_W), y_dtype)
    st_in = pl.BlockSpec((None, nbb, BRANCH_W, RWKV_HEAD), lambda i, j: (sl, i, 0, 0))
    sh_in = pl.BlockSpec((None, nbb, RWKV_SHIFT_W), lambda i, j: (sl, i, 0))
    st_spec = pl.BlockSpec((nbb, BRANCH_W, RWKV_HEAD), lambda i, j: (i, 0, 0))
    sh_spec = pl.BlockSpec((nbb, RWKV_SHIFT_W), lambda i, j: (i, 0))
    perm, permt, tri = _rw_constants(nt, nbb, part)
    lw = lambda name: _weight_operand(wl[name], l)
    const = lambda a: (a, _full_spec(a.shape))
    operands = (lw("g_pre"), lw("w_rw"), const(perm), const(permt), lw("rw_mu"), lw("rw_w0"), lw("rw_lora"),
                lw("rw_a0"), lw("rw_kk"), lw("rw_ka"), lw("rw_rk"), lw("rw_gnw"), lw("rw_gnb"),
                const(wl["rw_ones"]), const(tri))
    weights = [a for a, _ in operands]
    work = lambda dt: pltpu.VMEM((rows, BRANCH_W), dt)
    return pl.pallas_call(
        functools.partial(_rw_body, nt=nt, nbb=nbb, part=part, gw=gw, w_off=wl["w_rw"].offset),
        grid=grid,
        in_specs=[x_spec, st_in, sh_in] + [spec for _, spec in operands],
        out_specs=[y_spec, st_spec, sh_spec],
        out_shape=[y_shape, jax.ShapeDtypeStruct((batch, BRANCH_W, RWKV_HEAD), F32),
                   jax.ShapeDtypeStruct((batch, RWKV_SHIFT_W), F32)],
        scratch_shapes=[pltpu.VMEM((nbb, BRANCH_W // gw, gw, gw), F32),
                        pltpu.VMEM((nbb, BRANCH_W // gw, gw, gw), BF16),
                        pltpu.VMEM((nbb, RWKV_SHIFT_W), F32),
                        work(BF16), work(F32), work(BF16), work(BF16), work(BF16), work(F32), work(F32),
                        work(F32), work(F32)],
        compiler_params=_params(("arbitrary", "arbitrary")),
    )(x, s0, shift0, *weights)


def _merge_body(x_ref, ys5_ref, yrw_ref, ypl_ref, gpre_ref, wg_ref, wu_ref, wo_ref, gpost_ref, o_ref):
    x = x_ref[...]
    h = _rms_scale(x, gpre_ref[...]).astype(BF16)
    merged = None
    for i, y_ref in enumerate((ys5_ref, yrw_ref, ypl_ref)):
        gate = _sigmoid(_dot(h, wg_ref[:, D_MODEL * i:D_MODEL * (i + 1)]))
        term = gate * _dot(y_ref[...].astype(BF16), wu_ref[i])
        merged = term if merged is None else merged + term
    out = _dot(merged.astype(BF16), wo_ref[...])
    o_ref[...] = x + _rms_scale(out, gpost_ref[...])


def _merge_call(x, ys5, yrw, ypl, wl, *, l, tm):
    m = x.shape[0]
    tm = min(tm, m)
    weights, w_specs = zip(*[_weight_operand(wl[k], l) for k in ("g_pre", "w_gate", "w_up", "w_o", "g_post")])
    row_spec = lambda w: pl.BlockSpec((tm, w), lambda i: (i, 0))
    return pl.pallas_call(
        _merge_body,
        grid=(m // tm,),
        in_specs=[row_spec(D_MODEL), row_spec(BRANCH_W), row_spec(BRANCH_W), row_spec(BRANCH_W)]
        + list(w_specs),
        out_specs=row_spec(D_MODEL),
        out_shape=jax.ShapeDtypeStruct((m, D_MODEL), F32),
        compiler_params=_params(("parallel",)),
    )(x, ys5, yrw, ypl, *weights)


def _block_diag(blocks):
    *lead, n, r, c = blocks.shape
    eye = jnp.eye(n, dtype=blocks.dtype)
    return jnp.einsum("...grc,gh->...grhc", blocks, eye).reshape(*lead, n * r, n * c)


def _prep_weights(norm_pre, norm_post, w_in, s5_lam_re, s5_lam_im, s5_log_dt, s5_b_re, s5_b_im,
                  s5_c_re, s5_c_im, s5_d, s5_w_glu, rwkv_mu, rwkv_w0, rwkv_w_w2, rwkv_a0, rwkv_w_a2,
                  rwkv_k_k, rwkv_k_a, rwkv_r_k, rwkv_gn_w, rwkv_gn_b, pool_w, pool_scale,
                  w_up_s5, w_up_rwkv, w_up_pool, w_o):
    depth = w_in.shape[0]
    row = lambda a: a.astype(F32).reshape(depth, 1, -1)
    wl = {}
    wl["g_pre"] = row(norm_pre)
    wl["g_post"] = row(norm_post)
    w16 = w_in.astype(BF16)
    wl["w_s5"] = _ColWindow(w16, 2 * BRANCH_W, 0, 0)
    wl["w_rw"] = _ColWindow(w16, 3200, 0, 1024)
    wl["w_pool"] = _ColWindow(w16, 1408, 2, 3200 - 2 * 1408)
    wl["w_gate"] = w16[:, :, 4224:7296]

    lr = s5_lam_re.astype(F32)
    li = s5_lam_im.astype(F32)
    dt = jnp.exp(s5_log_dt.astype(F32))[:, :, None]
    mag = jnp.exp(lr * dt)
    ab_r = mag * jnp.cos(li * dt)
    ab_i = mag * jnp.sin(li * dt)
    den = lr * lr + li * li
    nr = ab_r - 1.0
    co_r = ((nr * lr + ab_i * li) / den)[..., None]
    co_i = ((ab_i * lr - nr * li) / den)[..., None]
    b_re = s5_b_re.astype(F32)
    b_im = s5_b_im.astype(F32)
    bf_re = co_r * b_re - co_i * b_im
    bf_im = co_r * b_im + co_i * b_re
    gh = S5_GROUPS // 2
    wb = []
    for k in range(2):
        gs = slice(gh * k, gh * (k + 1))
        wb.append(jnp.concatenate([_block_diag(jnp.swapaxes(bf_re[:, gs], 2, 3)),
                                   _block_diag(jnp.swapaxes(bf_im[:, gs], 2, 3))], axis=2))
    wl["s5_wb"] = jnp.stack(wb, axis=1).astype(BF16)
    wl["s5_ar"] = ab_r.reshape(depth, 1, S5_NSTATE)
    wl["s5_ai"] = ab_i.reshape(depth, 1, S5_NSTATE)
    c_re = s5_c_re.astype(F32)
    c_im = s5_c_im.astype(F32)
    cc = []
    for k in range(2):
        gs = slice(gh * k, gh * (k + 1))
        cc.append(jnp.concatenate([_block_diag(jnp.swapaxes(c_re[:, gs], 2, 3)),
                                   _block_diag(jnp.swapaxes(-c_im[:, gs], 2, 3))], axis=1))
    wl["s5_cc"] = jnp.stack(cc, axis=1).astype(BF16)
    wl["s5_d"] = row(s5_d)
    wl["s5_wglu"] = s5_w_glu.astype(BF16)

    wl["rw_mu"] = row(rwkv_mu)
    wl["rw_w0"] = row(rwkv_w0)
    zeros = jnp.zeros((depth, LORA_RANK, BRANCH_W), F32)
    wl["rw_lora"] = jnp.concatenate(
        [jnp.concatenate([rwkv_w_w2.astype(F32), zeros], axis=2),
         jnp.concatenate([zeros, rwkv_w_a2.astype(F32)], axis=2)], axis=1).astype(BF16)
    wl["rw_a0"] = row(rwkv_a0)
    wl["rw_kk"] = row(rwkv_k_k)
    wl["rw_ka"] = row(rwkv_k_a)
    wl["rw_rk"] = row(rwkv_r_k)
    wl["rw_gnw"] = row(rwkv_gn_w)
    wl["rw_gnb"] = row(rwkv_gn_b)
    wl["rw_ones"] = _block_diag(jnp.ones((MXU_TILE // RWKV_HEAD, RWKV_HEAD, RWKV_HEAD), F32)).astype(BF16)

    wl["pool_w"] = _block_diag(pool_w.astype(F32)).astype(BF16)
    wl["pool_scale"] = row(pool_scale)
    wl["w_up"] = jnp.stack([w_up_s5, w_up_rwkv, w_up_pool], axis=1).astype(BF16)
    wl["w_o"] = w_o.astype(BF16)
    return wl


def _layer(x, states, wl, *, l, sl, seq, batch, start, prompt):
    s5r, s5i, rws, rwsh, pbuf = states
    if prompt:
        nt, nb, nbb, y_dtype = CHUNK, batch, batch, BF16
        nts = min(seq, 2 * CHUNK)
    else:
        nt, nb, nbb, y_dtype = seq, min(batch, 32), min(batch, 16), F32
        nts = seq
    x3 = x.reshape(seq, batch, D_MODEL)
    flat = lambda y: y.reshape(seq * batch, BRANCH_W)
    if nb == batch:
        x_tm = x
        pbuf_in = pbuf.reshape(pbuf.shape[0], POOL_BUF * batch, BRANCH_W)
    else:
        x_tm = x3
        pbuf_in = pbuf
    ys5, n_re, n_im = _s5_call(x_tm, wl, s5r, s5i, l=l, sl=sl, seq=seq, batch=batch, nt=nts, nb=nb, y_dtype=y_dtype)
    ypl, n_buf = _pool_call(x_tm, wl, pbuf_in, l=l, sl=sl, seq=seq, batch=batch, nt=nts, nb=nb, start=start,
                            y_dtype=y_dtype)
    yrw, n_s, n_shift = _rw_call(x if nbb == batch else x3, rwsh, rws, wl, l=l, sl=sl, seq=seq, batch=batch,
                                 nt=nt, nbb=nbb, y_dtype=y_dtype)
    x_new = _merge_call(x, flat(ys5), flat(yrw), flat(ypl), wl, l=l, tm=1024)
    n_buf = n_buf.reshape(POOL_BUF, batch, BRANCH_W)
    return x_new, (n_re, n_im, n_s, n_shift, n_buf)


def kernel(x_prompt, x_sample, state_s5_re, state_s5_im, state_rwkv, state_shift, state_pool, norm_pre, norm_post, w_in, s5_lam_re, s5_lam_im, s5_log_dt, s5_b_re, s5_b_im, s5_c_re, s5_c_im, s5_d, s5_w_glu, rwkv_mu, rwkv_w0, rwkv_w_w2, rwkv_a0, rwkv_w_a2, rwkv_k_k, rwkv_k_a, rwkv_r_k, rwkv_gn_w, rwkv_gn_b, pool_w, pool_scale, w_up_s5, w_up_rwkv, w_up_pool, w_o):
    depth = w_in.shape[0]
    bp, tp, _ = x_prompt.shape
    bs, ts, _ = x_sample.shape
    xp = jnp.transpose(x_prompt, (1, 0, 2)).reshape(tp * bp, D_MODEL)
    xs = jnp.transpose(x_sample, (1, 0, 2)).reshape(ts * bs, D_MODEL)
    outs_p = [[] for _ in range(5)]
    outs_s = [[] for _ in range(5)]
    wl = _prep_weights(norm_pre, norm_post, w_in, s5_lam_re, s5_lam_im, s5_log_dt, s5_b_re, s5_b_im,
                       s5_c_re, s5_c_im, s5_d, s5_w_glu, rwkv_mu, rwkv_w0, rwkv_w_w2, rwkv_a0, rwkv_w_a2,
                       rwkv_k_k, rwkv_k_a, rwkv_r_k, rwkv_gn_w, rwkv_gn_b, pool_w, pool_scale,
                       w_up_s5, w_up_rwkv, w_up_pool, w_o)
    zeros_p = (jnp.zeros((1, bp, S5_NSTATE), F32), jnp.zeros((1, bp, S5_NSTATE), F32),
               jnp.zeros((1, bp, BRANCH_W, RWKV_HEAD), F32), jnp.zeros((1, bp, RWKV_SHIFT_W), F32),
               jnp.zeros((1, POOL_BUF, bp, BRANCH_W), F32))
    for l in range(depth):
        xp, new_p = _layer(xp, zeros_p, wl, l=l, sl=0, seq=tp, batch=bp, start=0, prompt=True)
        st_s = (state_s5_re[l].reshape(1, bs, S5_NSTATE), state_s5_im[l].reshape(1, bs, S5_NSTATE),
                state_rwkv[l].reshape(1, bs, BRANCH_W, RWKV_HEAD), state_shift[l][None],
                jnp.transpose(state_pool[l], (1, 0, 2))[None])
        xs, new_s = _layer(xs, st_s, wl, l=l, sl=0, seq=ts, batch=bs, start=PAST_LEN, prompt=False)
        for lst, val in zip(outs_p, new_p):
            lst.append(val)
        for lst, val in zip(outs_s, new_s):
            lst.append(val)

    def finish(outs, b):
        re, im, rw, sh, pb = [jnp.stack(v) for v in outs]
        return (re.reshape(depth, b, S5_GROUPS, S5_STATE), im.reshape(depth, b, S5_GROUPS, S5_STATE),
                rw.reshape(depth, b, RWKV_HEADS, RWKV_HEAD, RWKV_HEAD), sh,
                jnp.transpose(pb, (0, 2, 1, 3)))

    y_p = jnp.transpose(xp.reshape(tp, bp, D_MODEL), (1, 0, 2))
    y_s = jnp.transpose(xs.reshape(ts, bs, D_MODEL), (1, 0, 2))
    return (y_p, y_s) + finish(outs_p, bp) + finish(outs_s, bs)
```

```python
import functools
import math
from typing import NamedTuple

import jax
import jax.numpy as jnp
import numpy as np
from jax import lax
from jax.experimental import pallas as pl
from jax.experimental.pallas import tpu as pltpu

F32 = jnp.float32
BF16 = jnp.bfloat16

D_MODEL = 1024
BRANCH_W = 512
S5_GROUPS = 32
S5_STATE = 64
S5_NSTATE = S5_GROUPS * S5_STATE
RWKV_HEADS = 8
RWKV_HEAD = 64
LORA_RANK = 64
RWKV_SHIFT_W = 3 * BRANCH_W + 2 * LORA_RANK
POOL_WINDOWS = (2, 4, 8, 16)
POOL_GW = 128
POOL_BUF = 15
NORM_EPS = 1e-6
GN_EPS = 64e-5
PAST_LEN = 16384

MXU_TILE = 256
LANES = 128
CHUNK = 64
RW_GROUP = 256
RW_UNROLL = 8
VMEM_LIMIT = 56 * 1024 * 1024


def _dot(a, b):
    return jnp.dot(a, b, preferred_element_type=F32)


def _dot_nt(a, b):
    return lax.dot_general(a, b, (((1,), (1,)), ((), ())), preferred_element_type=F32)


def _dot_tn(a, b):
    return lax.dot_general(a, b, (((0,), (0,)), ((), ())), preferred_element_type=F32)


def _split3(x):
    hi = x.astype(BF16)
    r1 = x - hi.astype(F32)
    mid = r1.astype(BF16)
    lo = (r1 - mid.astype(F32)).astype(BF16)
    return hi, mid, lo


def _rms_scale(x, g):
    ms = jnp.mean(x * x, axis=-1, keepdims=True)
    return x * lax.rsqrt(ms + NORM_EPS) * g


def _sigmoid(x):
    return jax.nn.sigmoid(x)


def _silu(x):
    return x * jax.nn.sigmoid(x)


def _gelu_tanh(x):
    c = math.sqrt(2.0 / math.pi)
    return 0.5 * x * (1.0 + jnp.tanh(c * (x + 0.044715 * (x * x * x))))


def _full_spec(shape):
    nd = len(shape)
    return pl.BlockSpec(shape, lambda *_: (0,) * nd)


def _layer_spec(shape, l):
    nd = len(shape) - 1
    return pl.BlockSpec((None,) + tuple(shape[1:]), lambda *_: (l,) + (0,) * nd)


class _ColWindow(NamedTuple):
    array: jax.Array
    block_cols: int
    block_index: int
    offset: int


def _weight_operand(w, l):
    if isinstance(w, _ColWindow):
        rows = w.array.shape[1]
        return w.array, pl.BlockSpec((None, rows, w.block_cols), lambda *_: (l, 0, w.block_index))
    return w, _layer_spec(w.shape, l)


def _params(sem):
    return pltpu.CompilerParams(dimension_semantics=sem, vmem_limit_bytes=VMEM_LIMIT)


def _s5_body(x_ref, g_ref, w_ref, wb_ref, ar_ref, ai_ref, cc_ref, d_ref, wglu_ref, h0r_ref, h0i_ref,
             y_ref, nr_ref, ni_ref, bbr, bbi, cr, ci, *, nt, nb):
    j = pl.program_id(1)
    rows = nt * nb
    half = S5_NSTATE // 2

    @pl.when(j == 0)
    def _():
        cr[...] = h0r_ref[...]
        ci[...] = h0i_ref[...]

    x = x_ref[...].reshape(rows, D_MODEL)
    h = _rms_scale(x, g_ref[...]).astype(BF16)
    sxz = _dot(h, w_ref[...])
    sx = sxz[:, :BRANCH_W]
    sz = sxz[:, BRANCH_W:]
    ub = sx.astype(BF16)
    for k in range(2):
        o = _dot(ub[:, MXU_TILE * k:MXU_TILE * (k + 1)], wb_ref[k])
        bbr[:, half * k:half * (k + 1)] = o[:, :half]
        bbi[:, half * k:half * (k + 1)] = o[:, half:]

    cw = max(128, min(1024, (8 * 1024) // nb))
    for c0 in range(0, S5_NSTATE, cw):
        cs = slice(c0, c0 + cw)
        a_r = jnp.broadcast_to(ar_ref[:, cs], (nb, cw))
        a_i = jnp.broadcast_to(ai_ref[:, cs], (nb, cw))

        def step(t, carry, cs=cs, a_r=a_r, a_i=a_i):
            hr, hi = carry
            rs = pl.ds(pl.multiple_of(t * nb, nb), nb)
            nhr = a_r * hr - a_i * hi + bbr[rs, cs]
            nhi = a_r * hi + a_i * hr + bbi[rs, cs]
            bbr[rs, cs] = nhr
            bbi[rs, cs] = nhi
            return nhr, nhi

        hr, hi = lax.fori_loop(0, nt, step, (cr[:, cs], ci[:, cs]), unroll=min(nt, 8))
        cr[:, cs] = hr
        ci[:, cs] = hi

    ys = []
    for k in range(2):
        ks = slice(half * k, half * (k + 1))
        lhs = jnp.concatenate([bbr[:, ks].astype(BF16), bbi[:, ks].astype(BF16)], axis=1)
        ys.append(_dot(lhs, cc_ref[k]))
    y = jnp.concatenate(ys, axis=1) + d_ref[...] * sx
    y = _gelu_tanh(y)
    y = y * _sigmoid(_dot(y.astype(BF16), wglu_ref[...]))
    y_ref[...] = (y * _silu(sz)).astype(y_ref.dtype).reshape(y_ref.shape)
    nr_ref[...] = cr[...]
    ni_ref[...] = ci[...]


def _s5_call(x, wl, h0r, h0i, *, l, sl, seq, batch, nt, nb, y_dtype):
    rows = nt * nb
    grid = (batch // nb, seq // nt)
    if nb == batch:
        x_spec = pl.BlockSpec((rows, D_MODEL), lambda i, j: (j, 0))
        y_spec = pl.BlockSpec((rows, BRANCH_W), lambda i, j: (j, 0))
        y_shape = jax.ShapeDtypeStruct((seq * batch, BRANCH_W), y_dtype)
    else:
        x_spec = pl.BlockSpec((nt, nb, D_MODEL), lambda i, j: (j, i, 0))
        y_spec = pl.BlockSpec((nt, nb, BRANCH_W), lambda i, j: (j, i, 0))
        y_shape = jax.ShapeDtypeStruct((seq, batch, BRANCH_W), y_dtype)
    st_in = pl.BlockSpec((None, nb, S5_NSTATE), lambda i, j: (sl, i, 0))
    st_spec = pl.BlockSpec((nb, S5_NSTATE), lambda i, j: (i, 0))
    st_shape = jax.ShapeDtypeStruct((batch, S5_NSTATE), F32)
    weights, w_specs = zip(*[_weight_operand(wl[k], l) for k in (
        "g_pre", "w_s5", "s5_wb", "s5_ar", "s5_ai", "s5_cc", "s5_d", "s5_wglu")])
    return pl.pallas_call(
        functools.partial(_s5_body, nt=nt, nb=nb),
        grid=grid,
        in_specs=[x_spec] + list(w_specs) + [st_in, st_in],
        out_specs=[y_spec, st_spec, st_spec],
        out_shape=[y_shape, st_shape, st_shape],
        scratch_shapes=[pltpu.VMEM((rows, S5_NSTATE), F32), pltpu.VMEM((rows, S5_NSTATE), F32),
                        pltpu.VMEM((nb, S5_NSTATE), F32), pltpu.VMEM((nb, S5_NSTATE), F32)],
        compiler_params=_params(("arbitrary", "arbitrary")),
    )(x, *weights, h0r, h0i)


def _pool_body(x_ref, g_ref, w_ref, pw_ref, ps_ref, buf_ref, y_ref, nbuf_ref, ext, *, nt, nb, start, w_off):
    j = pl.program_id(1)
    rows = nt * nb
    prev = POOL_BUF * nb

    @pl.when(j == 0)
    def _():
        ext[0:prev, :] = buf_ref[...].reshape(prev, BRANCH_W)

    x = x_ref[...].reshape(rows, D_MODEL)
    h = _rms_scale(x, g_ref[...]).astype(BF16)
    pxz = _dot(h, w_ref[:, w_off:])
    px = pxz[:, :BRANCH_W]
    pz = pxz[:, BRANCH_W:]
    ext[prev:prev + rows, :] = px

    t_local = lax.broadcasted_iota(jnp.int32, (rows, POOL_GW), 0) // nb
    pos1 = start + j * nt + t_local + 1
    outs = []
    for gi, win in enumerate(POOL_WINDOWS):
        cs = slice(POOL_GW * gi, POOL_GW * (gi + 1))
        acc = ext[prev:prev + rows, cs]
        for q in range(1, win):
            acc = acc + ext[prev - q * nb:prev - q * nb + rows, cs]
        cnt = jnp.minimum(pos1, win).astype(F32)
        outs.append(acc / cnt)
    pooled = jnp.concatenate(outs, axis=1) - px
    mixed = _dot(pooled.astype(BF16), pw_ref[...]) * ps_ref[...]
    y_ref[...] = (mixed * _silu(pz)).astype(y_ref.dtype).reshape(y_ref.shape)
    last = ext[rows:rows + prev, :]
    nbuf_ref[...] = last.reshape(nbuf_ref.shape)
    ext[0:prev, :] = last


def _pool_call(x, wl, buf, *, l, sl, seq, batch, nt, nb, start, y_dtype):
    rows = nt * nb
    grid = (batch // nb, seq // nt)
    if nb == batch:
        x_spec = pl.BlockSpec((rows, D_MODEL), lambda i, j: (j, 0))
        y_spec = pl.BlockSpec((rows, BRANCH_W), lambda i, j: (j, 0))
        y_shape = jax.ShapeDtypeStruct((seq * batch, BRANCH_W), y_dtype)
        b_in = pl.BlockSpec((None, POOL_BUF * nb, BRANCH_W), lambda i, j: (sl, 0, 0))
        b_spec = pl.BlockSpec((POOL_BUF * nb, BRANCH_W), lambda i, j: (0, 0))
        b_shape = jax.ShapeDtypeStruct((POOL_BUF * batch, BRANCH_W), F32)
    else:
        x_spec = pl.BlockSpec((nt, nb, D_MODEL), lambda i, j: (j, i, 0))
        y_spec = pl.BlockSpec((nt, nb, BRANCH_W), lambda i, j: (j, i, 0))
        y_shape = jax.ShapeDtypeStruct((seq, batch, BRANCH_W), y_dtype)
        b_in = pl.BlockSpec((None, POOL_BUF, nb, BRANCH_W), lambda i, j: (sl, 0, i, 0))
        b_spec = pl.BlockSpec((POOL_BUF, nb, BRANCH_W), lambda i, j: (0, i, 0))
        b_shape = jax.ShapeDtypeStruct((POOL_BUF, batch, BRANCH_W), F32)
    weights, w_specs = zip(*[_weight_operand(wl[k], l) for k in ("g_pre", "w_pool", "pool_w", "pool_scale")])
    return pl.pallas_call(
        functools.partial(_pool_body, nt=nt, nb=nb, start=start, w_off=wl["w_pool"].offset),
        grid=grid,
        in_specs=[x_spec] + list(w_specs) + [b_in],
        out_specs=[y_spec, b_spec],
        out_shape=[y_shape, b_shape],
        scratch_shapes=[pltpu.VMEM(((POOL_BUF + nt) * nb, BRANCH_W), F32)],
        compiler_params=_params(("arbitrary", "arbitrary")),
    )(x, *weights, buf)


def _seg_sums(xs, ones):
    rows = xs[0].shape[0]
    stacked = jnp.concatenate([x.astype(BF16) for x in xs], axis=0)
    s = jnp.concatenate([_dot(stacked[:, MXU_TILE * g:MXU_TILE * (g + 1)], ones)
                         for g in range(BRANCH_W // MXU_TILE)], axis=1)
    return [s[i * rows:(i + 1) * rows] for i in range(len(xs))]


def _block_rows(y, head_masks):
    yb = y.astype(BF16)
    return jnp.concatenate([yb * m for m in head_masks], axis=0)


def _rw_body(x_ref, s0_ref, sh_ref, g_ref, w_ref, perm_ref, permt_ref, mu_ref, w0_ref, wl_ref, a0_ref,
             kkw_ref, kaw_ref, rkw_ref, gnw_ref, gnb_ref, ones_ref, tri_ref,
             y_ref, sout_ref, shout_ref,
             sbd, sbb, carry, qk_s, qr_s, kb_s, kk_s, v_s, ei_s, bonus_s, z_s, o_s,
             *, nt, nbb, part, gw, w_off):
    j = pl.program_id(1)
    rows = nt * nbb
    n_ch = rows // CHUNK
    spc = CHUNK // nt
    log_seq = nt.bit_length() - 1

    n_grp = BRANCH_W // gw
    grp_heads = gw // RWKV_HEAD

    lane = lax.broadcasted_iota(jnp.int32, (CHUNK, gw), 1)
    row64 = lax.broadcasted_iota(jnp.int32, (CHUNK, gw), 0)
    src = lane & (CHUNK - 1)
    same_seq = (row64 >> log_seq) == (src >> log_seq)
    m_strict = same_seq & (src < row64)
    m_incl = same_seq & (src <= row64)
    eye_sbs = (src == row64).astype(F32)
    head_masks = [((lane >> 6) == hh).astype(BF16) for hh in range(grp_heads)]
    half_lane = lax.broadcasted_iota(jnp.int32, (RWKV_HEAD, LANES), 1)
    half_masks = [half_lane < RWKV_HEAD, half_lane >= RWKV_HEAD]

    @pl.when(j == 0)
    def _():
        zero_blk = jnp.zeros((RWKV_HEAD, RWKV_HEAD), F32)
        for q in range(nbb):
            for g in range(n_grp):
                sbd[q, g] = jnp.zeros((gw, gw), F32)
                sbb[q, g] = jnp.zeros((gw, gw), BF16)
                for hh in range(grp_heads):
                    hr = slice(RWKV_HEAD * hh, RWKV_HEAD * (hh + 1))
                    hl = slice(LANES * (hh // 2), LANES * (hh // 2 + 1))
                    blk = s0_ref[q, gw * g + RWKV_HEAD * hh:gw * g + RWKV_HEAD * (hh + 1), :]
                    piece = jnp.concatenate([blk, zero_blk] if hh % 2 == 0 else [zero_blk, blk], axis=1)
                    sbd[q, g, hr, hl] = piece
                    sbb[q, g, hr, hl] = piece.astype(BF16)
        carry[...] = sh_ref[...]

    x = x_ref[...].reshape(rows, D_MODEL)
    h = _rms_scale(x, g_ref[...]).astype(BF16)
    hb = _dot(perm_ref[...], h).astype(BF16)
    ones = ones_ref[...]

    is_first = (lax.broadcasted_iota(jnp.int32, (part, RWKV_SHIFT_W), 0) & (nt - 1)) == 0
    lane128 = lax.broadcasted_iota(jnp.int32, (part, 2 * LORA_RANK), 1)

    def prepare(r0):
        rs = slice(r0, r0 + part)
        q0 = r0 // nt
        pz = _dot(hb[rs], w_ref[:, w_off:])
        yield
        p = pz[:, :RWKV_SHIFT_W]
        z_s[rs, :] = pz[:, RWKV_SHIFT_W:]
        first = jnp.concatenate(
            [jnp.broadcast_to(carry[q0 + q:q0 + q + 1, :], (nt, RWKV_SHIFT_W)) for q in range(part // nt)], axis=0)
        prev = jnp.where(is_first, first, pltpu.roll(p, 1, 0))
        for q in range(part // nt):
            carry[q0 + q:q0 + q + 1, :] = p[nt * (q + 1) - 1:nt * (q + 1), :]
        pm = p + (prev - p) * mu_ref[...]
        yield
        r = pm[:, 0:BRANCH_W]
        k = pm[:, BRANCH_W:2 * BRANCH_W]
        v = pm[:, 2 * BRANCH_W:3 * BRANCH_W]
        ll = pm[:, 3 * BRANCH_W:]
        xl = jnp.where(lane128 < LORA_RANK, jnp.tanh(ll), ll).astype(BF16)
        dl = _dot(xl, wl_ref[...])
        yield
        w_log = -jax.nn.softplus(-(w0_ref[...] + dl[:, :BRANCH_W])) - 0.5
        ld = -jnp.exp(w_log)
        yield
        a = _sigmoid(a0_ref[...] + dl[:, BRANCH_W:])
        kk = k * kkw_ref[...]
        k2 = k * (1.0 + (a - 1.0) * kaw_ref[...])
        yield
        ssq, rk = _seg_sums([kk * kk, r * k2 * rkw_ref[...]], ones)
        hi, mid, lo = _split3(ld)
        cum3 = _dot(tri_ref[...], jnp.concatenate([hi, mid, lo], axis=1))
        yield
        kkn = kk * lax.rsqrt(jnp.maximum(ssq, 1e-24))
        cum = cum3[:, :BRANCH_W] + cum3[:, BRANCH_W:2 * BRANCH_W] + cum3[:, 2 * BRANCH_W:]
        e_incl = jnp.exp(cum)
        yield
        e_inv = jnp.exp(-cum)
        yield
        qk_s[rs, :] = (kkn * jnp.exp(cum - ld)).astype(BF16)
        qr_s[rs, :] = r * e_incl
        yield
        kb_s[rs, :] = (kkn * a * e_inv).astype(BF16)
        kk_s[rs, :] = (k2 * e_inv).astype(BF16)
        v_s[rs, :] = v.astype(BF16)
        ei_s[rs, :] = e_incl
        bonus_s[rs, :] = rk * v

    def chunk(c, g):
        r0 = c * CHUNK if isinstance(c, int) else pl.multiple_of(c * CHUNK, CHUNK)
        rs = pl.ds(r0, CHUNK)
        ls = slice(gw * g, gw * (g + 1))
        q_k = qk_s[rs, ls]
        q_r = qr_s[rs, ls]
        k_b = kb_s[rs, ls]
        k_k = kk_s[rs, ls]
        vb = v_s[rs, ls]
        q2 = jnp.concatenate([q_k, q_r.astype(BF16)], axis=0)
        bd_v = _block_rows(vb, head_masks)
        a_all = _dot_nt(q2, jnp.concatenate([_block_rows(k_b, head_masks), _block_rows(k_k, head_masks)], axis=0))
        yield
        a_kb = jnp.where(m_strict, a_all[:CHUNK, :gw], 0.0)
        a_rb = jnp.where(m_incl, a_all[CHUNK:, :gw], 0.0).astype(BF16)
        a_kk = jnp.where(m_strict, a_all[:CHUNK, gw:], 0.0).astype(BF16)
        a_rk = jnp.where(m_incl, a_all[CHUNK:, gw:], 0.0).astype(BF16)
        av = _dot(jnp.concatenate([a_kk, a_rk], axis=0), bd_v)
        npow = -a_kb
        tinv = eye_sbs + npow
        npow = _dot(npow.astype(BF16), _block_rows(npow, head_masks))
        yield
        akv = av[:CHUNK]
        o_v = av[CHUNK:]
        for it in range(1, log_seq):
            bd_n = _block_rows(npow, head_masks)
            if it + 1 < log_seq:
                both = _dot(jnp.concatenate([npow.astype(BF16), tinv.astype(BF16)], axis=0), bd_n)
                yield
                npow = both[:CHUNK]
                tinv = tinv + both[CHUNK:]
            else:
                last = _dot(tinv.astype(BF16), bd_n)
                yield
                tinv = tinv + last
        tw = _dot(tinv.astype(BF16),
                  jnp.concatenate([_block_rows(q_k, head_masks), _block_rows(-akv, head_masks)], axis=1))
        yield
        qt_k = tw[:, :gw]
        w_loc = tw[:, gw:]
        d1 = _dot(a_rb, jnp.concatenate([_block_rows(qt_k, head_masks), _block_rows(w_loc, head_masks)], axis=1))
        yield
        qh_r = q_r - d1[:, :gw]
        o_loc = d1[:, gw:] + o_v

        def update_state(sq, zz, pc):
            for hh in range(grp_heads):
                hr = slice(RWKV_HEAD * hh, RWKV_HEAD * (hh + 1))
                hl = slice(LANES * (hh // 2), LANES * (hh // 2 + 1))
                piece = (sbd[sq, g, hr, hl] + jnp.where(half_masks[hh % 2], zz[hr, hl], 0.0)) * pc[:, hl]
                sbd[sq, g, hr, hl] = piece
                sbb[sq, g, hr, hl] = piece.astype(BF16)

        if spc == 1:
            xs = _dot_nt(jnp.concatenate([qt_k.astype(BF16), qh_r.astype(BF16)], axis=0), sbb[c, g])
            yield
            u = w_loc - xs[:CHUNK]
            o_s[rs, ls] = xs[CHUNK:] + o_loc
            zz = _dot_tn(jnp.concatenate([u.astype(BF16), vb], axis=0), jnp.concatenate([k_b, k_k], axis=0))
            yield
            update_state(c, zz, ei_s[pl.ds(r0 + CHUNK - 1, 1), ls])
        else:
            kb_f = k_b.astype(F32)
            kk_f = k_k.astype(F32)
            v_f = vb.astype(F32)
            seqs = [slice(nt * q, nt * (q + 1)) for q in range(spc)]
            xs = [_dot_nt(jnp.concatenate([qt_k[qs], qh_r[qs]], axis=0).astype(BF16), sbb[c * spc + q, g])
                  for q, qs in enumerate(seqs)]
            yield
            o_s[rs, ls] = jnp.concatenate([x[nt:] for x in xs], axis=0) + o_loc
            zz = [_dot_tn(jnp.concatenate([w_loc[qs] - x[:nt], v_f[qs]], axis=0).astype(BF16),
                          jnp.concatenate([kb_f[qs], kk_f[qs]], axis=0).astype(BF16))
                  for x, qs in zip(xs, seqs)]
            yield
            for q in range(spc):
                update_state(c * spc + q, zz[q], ei_s[pl.ds(r0 + nt * (q + 1) - 1, 1), ls])

    def run_together(live):
        while live:
            still = []
            for gen in live:
                try:
                    next(gen)
                    still.append(gen)
                except StopIteration:
                    pass
            live = still

    run_together([prepare(r0) for r0 in range(0, rows, part)])
    if n_ch <= RW_UNROLL:
        run_together([chunk(c, g) for c in range(n_ch) for g in range(n_grp)])
    else:
        def loop_body(cc, carry_):
            run_together([chunk(cc * RW_UNROLL + i, g) for i in range(RW_UNROLL) for g in range(n_grp)])
            return carry_
        lax.fori_loop(0, n_ch // RW_UNROLL, loop_body, 0)

    ybs = []
    for r0 in range(0, rows, part):
        rs = slice(r0, r0 + part)
        o = o_s[rs, :]
        mean = _seg_sums([o], ones)[0] * (1.0 / RWKV_HEAD)
        dev = o - mean
        var = _seg_sums([dev * dev], ones)[0] * (1.0 / RWKV_HEAD)
        o = dev * lax.rsqrt(var + GN_EPS) * gnw_ref[...] + gnb_ref[...] + bonus_s[rs, :]
        ybs.append((o * _silu(z_s[rs, :])).astype(BF16))
    y_tm = _dot(permt_ref[...], jnp.concatenate(ybs, axis=0))
    y_ref[...] = y_tm.astype(y_ref.dtype).reshape(y_ref.shape)
    shout_ref[...] = carry[...]

    @pl.when(j == pl.num_programs(1) - 1)
    def _():
        for q in range(nbb):
            for g in range(n_grp):
                for hh in range(grp_heads):
                    hr = slice(RWKV_HEAD * hh, RWKV_HEAD * (hh + 1))
                    hl = slice(RWKV_HEAD * hh, RWKV_HEAD * (hh + 1))
                    sout_ref[q, gw * g + RWKV_HEAD * hh:gw * g + RWKV_HEAD * (hh + 1), :] = sbd[q, g, hr, hl]


def _rw_constants(nt, nbb, part):
    rows = nt * nbb
    r = np.arange(rows)
    src = (r % nt) * nbb + (r // nt)
    perm = np.zeros((rows, rows), np.float32)
    perm[r, src] = 1.0
    t = np.arange(part)
    tri = ((t[:, None] // nt) == (t[None, :] // nt)) & (t[None, :] <= t[:, None])
    return jnp.asarray(perm, BF16), jnp.asarray(perm.T, BF16), jnp.asarray(tri, BF16)


def _rw_call(x, shift0, s0, wl, *, l, sl, seq, batch, nt, nbb, y_dtype):
    rows = nt * nbb
    part = min(rows, 256)
    gw = RW_GROUP if nt == CHUNK else MXU_TILE
    grid = (batch // nbb, seq // nt)
    if nbb == batch:
        x_spec = pl.BlockSpec((rows, D_MODEL), lambda i, j: (j, 0))
        y_spec = pl.BlockSpec((rows, BRANCH_W), lambda i, j: (j, 0))
        y_shape = jax.ShapeDtypeStruct((seq * batch, BRANCH_W), y_dtype)
    else:
        x_spec = pl.BlockSpec((nt, nbb, D_MODEL), lambda i, j: (j, i, 0))
        y_spec = pl.BlockSpec((nt, nbb, BRANCH_W), lambda i, j: (j, i, 0))
        y_shape = jax.ShapeDtypeStruct((seq, batch, BRANCH_W), y_dtype)
    st_in = pl.BlockSpec((None, nbb, BRANCH_W, RWKV_HEAD), lambda i, j: (sl, i, 0, 0))
    sh_in = pl.BlockSpec((None, nbb, RWKV_SHIFT_W), lambda i, j: (sl, i, 0))
    st_spec = pl.BlockSpec((nbb, BRANCH_W, RWKV_HEAD), lambda i, j: (i, 0, 0))
    sh_spec = pl.BlockSpec((nbb, RWKV_SHIFT_W), lambda i, j: (i, 0))
    perm, permt, tri = _rw_constants(nt, nbb, part)
    lw = lambda name: _weight_operand(wl[name], l)
    const = lambda a: (a, _full_spec(a.shape))
    operands = (lw("g_pre"), lw("w_rw"), const(perm), const(permt), lw("rw_mu"), lw("rw_w0"), lw("rw_lora"),
                lw("rw_a0"), lw("rw_kk"), lw("rw_ka"), lw("rw_rk"), lw("rw_gnw"), lw("rw_gnb"),
                const(wl["rw_ones"]), const(tri))
    weights = [a for a, _ in operands]
    work = lambda dt: pltpu.VMEM((rows, BRANCH_W), dt)
    return pl.pallas_call(
        functools.partial(_rw_body, nt=nt, nbb=nbb, part=part, gw=gw, w_off=wl["w_rw"].offset),
        grid=grid,
        in_specs=[x_spec, st_in, sh_in] + [spec for _, spec in operands],
        out_specs=[y_spec, st_spec, sh_spec],
        out_shape=[y_shape, jax.ShapeDtypeStruct((batch, BRANCH_W, RWKV_HEAD), F32),
                   jax.ShapeDtypeStruct((batch, RWKV_SHIFT_W), F32)],
        scratch_shapes=[pltpu.VMEM((nbb, BRANCH_W // gw, gw, gw), F32),
                        pltpu.VMEM((nbb, BRANCH_W // gw, gw, gw), BF16),
                        pltpu.VMEM((nbb, RWKV_SHIFT_W), F32),
                        work(BF16), work(F32), work(BF16), work(BF16), work(BF16), work(F32), work(F32),
                        work(F32), work(F32)],
        compiler_params=_params(("arbitrary", "arbitrary")),
    )(x, s0, shift0, *weights)


def _merge_body(x_ref, ys5_ref, yrw_ref, ypl_ref, gpre_ref, wg_ref, wu_ref, wo_ref, gpost_ref, o_ref):
    x = x_ref[...]
    h = _rms_scale(x, gpre_ref[...]).astype(BF16)
    merged = None
    for i, y_ref in enumerate((ys5_ref, yrw_ref, ypl_ref)):
        gate = _sigmoid(_dot(h, wg_ref[:, D_MODEL * i:D_MODEL * (i + 1)]))
        term = gate * _dot(y_ref[...].astype(BF16), wu_ref[i])
        merged = term if merged is None else merged + term
    out = _dot(merged.astype(BF16), wo_ref[...])
    o_ref[...] = x + _rms_scale(out, gpost_ref[...])


def _merge_call(x, ys5, yrw, ypl, wl, *, l, tm):
    m = x.shape[0]
    tm = min(tm, m)
    weights, w_specs = zip(*[_weight_operand(wl[k], l) for k in ("g_pre", "w_gate", "w_up", "w_o", "g_post")])
    row_spec = lambda w: pl.BlockSpec((tm, w), lambda i: (i, 0))
    return pl.pallas_call(
        _merge_body,
        grid=(m // tm,),
        in_specs=[row_spec(D_MODEL), row_spec(BRANCH_W), row_spec(BRANCH_W), row_spec(BRANCH_W)]
        + list(w_specs),
        out_specs=row_spec(D_MODEL),
        out_shape=jax.ShapeDtypeStruct((m, D_MODEL), F32),
        compiler_params=_params(("parallel",)),
    )(x, ys5, yrw, ypl, *weights)


def _block_diag(blocks):
    *lead, n, r, c = blocks.shape
    eye = jnp.eye(n, dtype=blocks.dtype)
    return jnp.einsum("...grc,gh->...grhc", blocks, eye).reshape(*lead, n * r, n * c)


def _prep_weights(norm_pre, norm_post, w_in, s5_lam_re, s5_lam_im, s5_log_dt, s5_b_re, s5_b_im,
                  s5_c_re, s5_c_im, s5_d, s5_w_glu, rwkv_mu, rwkv_w0, rwkv_w_w2, rwkv_a0, rwkv_w_a2,
                  rwkv_k_k, rwkv_k_a, rwkv_r_k, rwkv_gn_w, rwkv_gn_b, pool_w, pool_scale,
                  w_up_s5, w_up_rwkv, w_up_pool, w_o):
    depth = w_in.shape[0]
    row = lambda a: a.astype(F32).reshape(depth, 1, -1)
    wl = {}
    wl["g_pre"] = row(norm_pre)
    wl["g_post"] = row(norm_post)
    c_rw = 2 * BRANCH_W
    c_pool = c_rw + RWKV_SHIFT_W + BRANCH_W
    c_gate = c_pool + 2 * BRANCH_W
    w16 = w_in.astype(BF16)

    def window(start, end):
        n = max(n for n in range(1, end // LANES + 1)
                if end % n == 0 and (end // n) % LANES == 0 and (n - 1) * (end // n) <= start)
        return _ColWindow(w16, end // n, n - 1, start - (n - 1) * (end // n))

    wl["w_s5"] = window(0, c_rw)
    wl["w_rw"] = window(c_rw, c_pool)
    wl["w_pool"] = window(c_pool, c_gate)
    wl["w_gate"] = w16[:, :, c_gate:]

    lr = s5_lam_re.astype(F32)
    li = s5_lam_im.astype(F32)
    dt = jnp.exp(s5_log_dt.astype(F32))[:, :, None]
    mag = jnp.exp(lr * dt)
    ab_r = mag * jnp.cos(li * dt)
    ab_i = mag * jnp.sin(li * dt)
    den = lr * lr + li * li
    nr = ab_r - 1.0
    co_r = ((nr * lr + ab_i * li) / den)[..., None]
    co_i = ((ab_i * lr - nr * li) / den)[..., None]
    b_re = s5_b_re.astype(F32)
    b_im = s5_b_im.astype(F32)
    bf_re = co_r * b_re - co_i * b_im
    bf_im = co_r * b_im + co_i * b_re
    gh = S5_GROUPS // 2
    wb = []
    for k in range(2):
        gs = slice(gh * k, gh * (k + 1))
        wb.append(jnp.concatenate([_block_diag(jnp.swapaxes(bf_re[:, gs], 2, 3)),
                                   _block_diag(jnp.swapaxes(bf_im[:, gs], 2, 3))], axis=2))
    wl["s5_wb"] = jnp.stack(wb, axis=1).astype(BF16)
    wl["s5_ar"] = ab_r.reshape(depth, 1, S5_NSTATE)
    wl["s5_ai"] = ab_i.reshape(depth, 1, S5_NSTATE)
    c_re = s5_c_re.astype(F32)
    c_im = s5_c_im.astype(F32)
    cc = []
    for k in range(2):
        gs = slice(gh * k, gh * (k + 1))
        cc.append(jnp.concatenate([_block_diag(jnp.swapaxes(c_re[:, gs], 2, 3)),
                                   _block_diag(jnp.swapaxes(-c_im[:, gs], 2, 3))], axis=1))
    wl["s5_cc"] = jnp.stack(cc, axis=1).astype(BF16)
    wl["s5_d"] = row(s5_d)
    wl["s5_wglu"] = s5_w_glu.astype(BF16)

    wl["rw_mu"] = row(rwkv_mu)
    wl["rw_w0"] = row(rwkv_w0)
    zeros = jnp.zeros((depth, LORA_RANK, BRANCH_W), F32)
    wl["rw_lora"] = jnp.concatenate(
        [jnp.concatenate([rwkv_w_w2.astype(F32), zeros], axis=2),
         jnp.concatenate([zeros, rwkv_w_a2.astype(F32)], axis=2)], axis=1).astype(BF16)
    wl["rw_a0"] = row(rwkv_a0)
    wl["rw_kk"] = row(rwkv_k_k)
    wl["rw_ka"] = row(rwkv_k_a)
    wl["rw_rk"] = row(rwkv_r_k)
    wl["rw_gnw"] = row(rwkv_gn_w)
    wl["rw_gnb"] = row(rwkv_gn_b)
    wl["rw_ones"] = _block_diag(jnp.ones((MXU_TILE // RWKV_HEAD, RWKV_HEAD, RWKV_HEAD), F32)).astype(BF16)

    wl["pool_w"] = _block_diag(pool_w.astype(F32)).astype(BF16)
    wl["pool_scale"] = row(pool_scale)
    wl["w_up"] = jnp.stack([w_up_s5, w_up_rwkv, w_up_pool], axis=1).astype(BF16)
    wl["w_o"] = w_o.astype(BF16)
    return wl


def _layer(x, states, wl, *, l, sl, seq, batch, start, prompt):
    s5r, s5i, rws, rwsh, pbuf = states
    if prompt:
        nt, nb, nbb, y_dtype = CHUNK, batch, batch, BF16
        nts = min(seq, 2 * CHUNK)
    else:
        nt, nb, nbb, y_dtype = seq, min(batch, 32), min(batch, 16), F32
        nts = seq
    x3 = x.reshape(seq, batch, D_MODEL)
    flat = lambda y: y.reshape(seq * batch, BRANCH_W)
    if nb == batch:
        x_tm = x
        pbuf_in = pbuf.reshape(pbuf.shape[0], POOL_BUF * batch, BRANCH_W)
    else:
        x_tm = x3
        pbuf_in = pbuf
    ys5, n_re, n_im = _s5_call(x_tm, wl, s5r, s5i, l=l, sl=sl, seq=seq, batch=batch, nt=nts, nb=nb, y_dtype=y_dtype)
    ypl, n_buf = _pool_call(x_tm, wl, pbuf_in, l=l, sl=sl, seq=seq, batch=batch, nt=nts, nb=nb, start=start,
                            y_dtype=y_dtype)
    yrw, n_s, n_shift = _rw_call(x if nbb == batch else x3, rwsh, rws, wl, l=l, sl=sl, seq=seq, batch=batch,
                                 nt=nt, nbb=nbb, y_dtype=y_dtype)
    x_new = _merge_call(x, flat(ys5), flat(yrw), flat(ypl), wl, l=l, tm=1024)
    n_buf = n_buf.reshape(POOL_BUF, batch, BRANCH_W)
    return x_new, (n_re, n_im, n_s, n_shift, n_buf)


def kernel(x_prompt, x_sample, state_s5_re, state_s5_im, state_rwkv, state_shift, state_pool, norm_pre, norm_post, w_in, s5_lam_re, s5_lam_im, s5_log_dt, s5_b_re, s5_b_im, s5_c_re, s5_c_im, s5_d, s5_w_glu, rwkv_mu, rwkv_w0, rwkv_w_w2, rwkv_a0, rwkv_w_a2, rwkv_k_k, rwkv_k_a, rwkv_r_k, rwkv_gn_w, rwkv_gn_b, pool_w, pool_scale, w_up_s5, w_up_rwkv, w_up_pool, w_o):
    depth = w_in.shape[0]
    bp, tp, _ = x_prompt.shape
    bs, ts, _ = x_sample.shape
    xp = jnp.transpose(x_prompt, (1, 0, 2)).reshape(tp * bp, D_MODEL)
    xs = jnp.transpose(x_sample, (1, 0, 2)).reshape(ts * bs, D_MODEL)
    outs_p = [[] for _ in range(5)]
    outs_s = [[] for _ in range(5)]
    wl = _prep_weights(norm_pre, norm_post, w_in, s5_lam_re, s5_lam_im, s5_log_dt, s5_b_re, s5_b_im,
                       s5_c_re, s5_c_im, s5_d, s5_w_glu, rwkv_mu, rwkv_w0, rwkv_w_w2, rwkv_a0, rwkv_w_a2,
                       rwkv_k_k, rwkv_k_a, rwkv_r_k, rwkv_gn_w, rwkv_gn_b, pool_w, pool_scale,
                       w_up_s5, w_up_rwkv, w_up_pool, w_o)
    zeros_p = (jnp.zeros((1, bp, S5_NSTATE), F32), jnp.zeros((1, bp, S5_NSTATE), F32),
               jnp.zeros((1, bp, BRANCH_W, RWKV_HEAD), F32), jnp.zeros((1, bp, RWKV_SHIFT_W), F32),
               jnp.zeros((1, POOL_BUF, bp, BRANCH_W), F32))
    for l in range(depth):
        xp, new_p = _layer(xp, zeros_p, wl, l=l, sl=0, seq=tp, batch=bp, start=0, prompt=True)
        st_s = (state_s5_re[l].reshape(1, bs, S5_NSTATE), state_s5_im[l].reshape(1, bs, S5_NSTATE),
                state_rwkv[l].reshape(1, bs, BRANCH_W, RWKV_HEAD), state_shift[l][None],
                jnp.transpose(state_pool[l], (1, 0, 2))[None])
        xs, new_s = _layer(xs, st_s, wl, l=l, sl=0, seq=ts, batch=bs, start=PAST_LEN, prompt=False)
        for lst, val in zip(outs_p, new_p):
            lst.append(val)
        for lst, val in zip(outs_s, new_s):
            lst.append(val)

    def finish(outs, b):
        re, im, rw, sh, pb = [jnp.stack(v) for v in outs]
        return (re.reshape(depth, b, S5_GROUPS, S5_STATE), im.reshape(depth, b, S5_GROUPS, S5_STATE),
                rw.reshape(depth, b, RWKV_HEADS, RWKV_HEAD, RWKV_HEAD), sh,
                jnp.transpose(pb, (0, 2, 1, 3)))

    y_p = jnp.transpose(xp.reshape(tp, bp, D_MODEL), (1, 0, 2))
    y_s = jnp.transpose(xs.reshape(ts, bs, D_MODEL), (1, 0, 2))
    return (y_p, y_s) + finish(outs_p, bp) + finish(outs_s, bs)
```

```python
import functools
import math
from typing import NamedTuple

import jax
import jax.numpy as jnp
import numpy as np
from jax import lax
from jax.experimental import pallas as pl
from jax.experimental.pallas import tpu as pltpu

F32 = jnp.float32
BF16 = jnp.bfloat16

D_MODEL = 1024
BRANCH_W = 512
S5_GROUPS = 32
S5_STATE = 64
S5_NSTATE = S5_GROUPS * S5_STATE
RWKV_HEADS = 8
RWKV_HEAD = 64
LORA_RANK = 64
RWKV_SHIFT_W = 3 * BRANCH_W + 2 * LORA_RANK
POOL_WINDOWS = (2, 4, 8, 16)
POOL_GW = 128
POOL_BUF = 15
NORM_EPS = 1e-6
GN_EPS = 64e-5
PAST_LEN = 16384

MXU_TILE = 256
LANES = 128
CHUNK = 64
RW_GROUP = 256
RW_UNROLL = 8
VMEM_LIMIT = 56 * 1024 * 1024


def _dot(a, b):
    return jnp.dot(a, b, preferred_element_type=F32)


def _dot_nt(a, b):
    return lax.dot_general(a, b, (((1,), (1,)), ((), ())), preferred_element_type=F32)


def _dot_tn(a, b):
    return lax.dot_general(a, b, (((0,), (0,)), ((), ())), preferred_element_type=F32)


def _split3(x):
    hi = x.astype(BF16)
    r1 = x - hi.astype(F32)
    mid = r1.astype(BF16)
    lo = (r1 - mid.astype(F32)).astype(BF16)
    return hi, mid, lo


def _rms_scale(x, g):
    ms = jnp.mean(x * x, axis=-1, keepdims=True)
    return x * lax.rsqrt(ms + NORM_EPS) * g


def _sigmoid(x):
    return jax.nn.sigmoid(x)


def _silu(x):
    return x * jax.nn.sigmoid(x)


def _gelu_tanh(x):
    c = math.sqrt(2.0 / math.pi)
    return 0.5 * x * (1.0 + jnp.tanh(c * (x + 0.044715 * (x * x * x))))


def _full_spec(shape):
    nd = len(shape)
    return pl.BlockSpec(shape, lambda *_: (0,) * nd)


def _layer_spec(shape, l):
    nd = len(shape) - 1
    return pl.BlockSpec((None,) + tuple(shape[1:]), lambda *_: (l,) + (0,) * nd)


class _ColWindow(NamedTuple):
    array: jax.Array
    block_cols: int
    block_index: int
    offset: int


def _weight_operand(w, l):
    if isinstance(w, _ColWindow):
        rows = w.array.shape[1]
        return w.array, pl.BlockSpec((None, rows, w.block_cols), lambda *_: (l, 0, w.block_index))
    return w, _layer_spec(w.shape, l)


def _params(sem):
    return pltpu.CompilerParams(dimension_semantics=sem, vmem_limit_bytes=VMEM_LIMIT)


def _s5_body(x_ref, g_ref, w_ref, wb_ref, ar_ref, ai_ref, cc_ref, d_ref, wglu_ref, h0r_ref, h0i_ref,
             y_ref, nr_ref, ni_ref, bbr, bbi, cr, ci, *, nt, nb):
    j = pl.program_id(1)
    rows = nt * nb
    half = S5_NSTATE // 2

    @pl.when(j == 0)
    def _():
        cr[...] = h0r_ref[...]
        ci[...] = h0i_ref[...]

    x = x_ref[...].reshape(rows, D_MODEL)
    h = _rms_scale(x, g_ref[...]).astype(BF16)
    sxz = _dot(h, w_ref[...])
    sx = sxz[:, :BRANCH_W]
    sz = sxz[:, BRANCH_W:]
    ub = sx.astype(BF16)
    for k in range(2):
        o = _dot(ub[:, MXU_TILE * k:MXU_TILE * (k + 1)], wb_ref[k])
        bbr[:, half * k:half * (k + 1)] = o[:, :half]
        bbi[:, half * k:half * (k + 1)] = o[:, half:]

    cw = max(128, min(1024, (8 * 1024) // nb))
    for c0 in range(0, S5_NSTATE, cw):
        cs = slice(c0, c0 + cw)
        a_r = jnp.broadcast_to(ar_ref[:, cs], (nb, cw))
        a_i = jnp.broadcast_to(ai_ref[:, cs], (nb, cw))

        hr, hi = cr[:, cs], ci[:, cs]
        for t in range(nt):
            rs = slice(t * nb, (t + 1) * nb)
            hr, hi = a_r * hr - a_i * hi + bbr[rs, cs], a_r * hi + a_i * hr + bbi[rs, cs]
            bbr[rs, cs] = hr
            bbi[rs, cs] = hi
        cr[:, cs] = hr
        ci[:, cs] = hi

    ys = []
    for k in range(2):
        ks = slice(half * k, half * (k + 1))
        lhs = jnp.concatenate([bbr[:, ks].astype(BF16), bbi[:, ks].astype(BF16)], axis=1)
        ys.append(_dot(lhs, cc_ref[k]))
    y = jnp.concatenate(ys, axis=1) + d_ref[...] * sx
    y = _gelu_tanh(y)
    y = y * _sigmoid(_dot(y.astype(BF16), wglu_ref[...]))
    y_ref[...] = (y * _silu(sz)).astype(y_ref.dtype).reshape(y_ref.shape)
    nr_ref[...] = cr[...]
    ni_ref[...] = ci[...]


def _s5_call(x, wl, h0r, h0i, *, l, sl, seq, batch, nt, nb, y_dtype):
    rows = nt * nb
    grid = (batch // nb, seq // nt)
    if nb == batch:
        x_spec = pl.BlockSpec((rows, D_MODEL), lambda i, j: (j, 0))
        y_spec = pl.BlockSpec((rows, BRANCH_W), lambda i, j: (j, 0))
        y_shape = jax.ShapeDtypeStruct((seq * batch, BRANCH_W), y_dtype)
    else:
        x_spec = pl.BlockSpec((nt, nb, D_MODEL), lambda i, j: (j, i, 0))
        y_spec = pl.BlockSpec((nt, nb, BRANCH_W), lambda i, j: (j, i, 0))
        y_shape = jax.ShapeDtypeStruct((seq, batch, BRANCH_W), y_dtype)
    st_in = pl.BlockSpec((None, nb, S5_NSTATE), lambda i, j: (sl, i, 0))
    st_spec = pl.BlockSpec((nb, S5_NSTATE), lambda i, j: (i, 0))
    st_shape = jax.ShapeDtypeStruct((batch, S5_NSTATE), F32)
    weights, w_specs = zip(*[_weight_operand(wl[k], l) for k in (
        "g_pre", "w_s5", "s5_wb", "s5_ar", "s5_ai", "s5_cc", "s5_d", "s5_wglu")])
    return pl.pallas_call(
        functools.partial(_s5_body, nt=nt, nb=nb),
        grid=grid,
        in_specs=[x_spec] + list(w_specs) + [st_in, st_in],
        out_specs=[y_spec, st_spec, st_spec],
        out_shape=[y_shape, st_shape, st_shape],
        scratch_shapes=[pltpu.VMEM((rows, S5_NSTATE), F32), pltpu.VMEM((rows, S5_NSTATE), F32),
                        pltpu.VMEM((nb, S5_NSTATE), F32), pltpu.VMEM((nb, S5_NSTATE), F32)],
        compiler_params=_params(("arbitrary", "arbitrary")),
    )(x, *weights, h0r, h0i)


def _pool_body(x_ref, g_ref, w_ref, pw_ref, ps_ref, buf_ref, y_ref, nbuf_ref, ext, *, nt, nb, start, w_off):
    j = pl.program_id(1)
    rows = nt * nb
    prev = POOL_BUF * nb

    @pl.when(j == 0)
    def _():
        ext[0:prev, :] = buf_ref[...].reshape(prev, BRANCH_W)

    x = x_ref[...].reshape(rows, D_MODEL)
    h = _rms_scale(x, g_ref[...]).astype(BF16)
    pxz = _dot(h, w_ref[:, w_off:])
    px = pxz[:, :BRANCH_W]
    pz = pxz[:, BRANCH_W:]
    ext[prev:prev + rows, :] = px

    t_local = lax.broadcasted_iota(jnp.int32, (rows, POOL_GW), 0) // nb
    pos1 = start + j * nt + t_local + 1
    outs = []
    for gi, win in enumerate(POOL_WINDOWS):
        cs = slice(POOL_GW * gi, POOL_GW * (gi + 1))
        acc = ext[prev:prev + rows, cs]
        for q in range(1, win):
            acc = acc + ext[prev - q * nb:prev - q * nb + rows, cs]
        cnt = jnp.minimum(pos1, win).astype(F32)
        outs.append(acc / cnt)
    pooled = jnp.concatenate(outs, axis=1) - px
    mixed = _dot(pooled.astype(BF16), pw_ref[...]) * ps_ref[...]
    y_ref[...] = (mixed * _silu(pz)).astype(y_ref.dtype).reshape(y_ref.shape)
    last = ext[rows:rows + prev, :]
    nbuf_ref[...] = last.reshape(nbuf_ref.shape)
    ext[0:prev, :] = last


def _pool_call(x, wl, buf, *, l, sl, seq, batch, nt, nb, start, y_dtype):
    rows = nt * nb
    grid = (batch // nb, seq // nt)
    if nb == batch:
        x_spec = pl.BlockSpec((rows, D_MODEL), lambda i, j: (j, 0))
        y_spec = pl.BlockSpec((rows, BRANCH_W), lambda i, j: (j, 0))
        y_shape = jax.ShapeDtypeStruct((seq * batch, BRANCH_W), y_dtype)
        b_in = pl.BlockSpec((None, POOL_BUF * nb, BRANCH_W), lambda i, j: (sl, 0, 0))
        b_spec = pl.BlockSpec((POOL_BUF * nb, BRANCH_W), lambda i, j: (0, 0))
        b_shape = jax.ShapeDtypeStruct((POOL_BUF * batch, BRANCH_W), F32)
    else:
        x_spec = pl.BlockSpec((nt, nb, D_MODEL), lambda i, j: (j, i, 0))
        y_spec = pl.BlockSpec((nt, nb, BRANCH_W), lambda i, j: (j, i, 0))
        y_shape = jax.ShapeDtypeStruct((seq, batch, BRANCH_W), y_dtype)
        b_in = pl.BlockSpec((None, POOL_BUF, nb, BRANCH_W), lambda i, j: (sl, 0, i, 0))
        b_spec = pl.BlockSpec((POOL_BUF, nb, BRANCH_W), lambda i, j: (0, i, 0))
        b_shape = jax.ShapeDtypeStruct((POOL_BUF, batch, BRANCH_W), F32)
    weights, w_specs = zip(*[_weight_operand(wl[k], l) for k in ("g_pre", "w_pool", "pool_w", "pool_scale")])
    return pl.pallas_call(
        functools.partial(_pool_body, nt=nt, nb=nb, start=start, w_off=wl["w_pool"].offset),
        grid=grid,
        in_specs=[x_spec] + list(w_specs) + [b_in],
        out_specs=[y_spec, b_spec],
        out_shape=[y_shape, b_shape],
        scratch_shapes=[pltpu.VMEM(((POOL_BUF + nt) * nb, BRANCH_W), F32)],
        compiler_params=_params(("arbitrary", "arbitrary")),
    )(x, *weights, buf)


def _seg_sums(xs, ones):
    rows = xs[0].shape[0]
    stacked = jnp.concatenate([x.astype(BF16) for x in xs], axis=0)
    s = jnp.concatenate([_dot(stacked[:, MXU_TILE * g:MXU_TILE * (g + 1)], ones)
                         for g in range(BRANCH_W // MXU_TILE)], axis=1)
    return [s[i * rows:(i + 1) * rows] for i in range(len(xs))]


def _block_rows(y, head_masks):
    yb = y.astype(BF16)
    return jnp.concatenate([yb * m for m in head_masks], axis=0)


def _rw_body(x_ref, s0_ref, sh_ref, g_ref, w_ref, perm_ref, permt_ref, mu_ref, w0_ref, wl_ref, a0_ref,
             kkw_ref, kaw_ref, rkw_ref, gnw_ref, gnb_ref, ones_ref, tri_ref,
             y_ref, sout_ref, shout_ref,
             sbd, sbb, carry, qk_s, qr_s, kb_s, kk_s, v_s, ei_s, bonus_s, z_s, o_s,
             *, nt, nbb, part, gw, w_off):
    j = pl.program_id(1)
    rows = nt * nbb
    n_ch = rows // CHUNK
    spc = CHUNK // nt
    log_seq = nt.bit_length() - 1

    n_grp = BRANCH_W // gw
    grp_heads = gw // RWKV_HEAD

    lane = lax.broadcasted_iota(jnp.int32, (CHUNK, gw), 1)
    row64 = lax.broadcasted_iota(jnp.int32, (CHUNK, gw), 0)
    src = lane & (CHUNK - 1)
    same_seq = (row64 >> log_seq) == (src >> log_seq)
    m_strict = same_seq & (src < row64)
    m_incl = same_seq & (src <= row64)
    eye_sbs = (src == row64).astype(F32)
    head_masks = [((lane >> 6) == hh).astype(BF16) for hh in range(grp_heads)]
    half_lane = lax.broadcasted_iota(jnp.int32, (RWKV_HEAD, LANES), 1)
    half_masks = [half_lane < RWKV_HEAD, half_lane >= RWKV_HEAD]

    @pl.when(j == 0)
    def _():
        zero_blk = jnp.zeros((RWKV_HEAD, RWKV_HEAD), F32)
        for q in range(nbb):
            for g in range(n_grp):
                sbd[q, g] = jnp.zeros((gw, gw), F32)
                sbb[q, g] = jnp.zeros((gw, gw), BF16)
                for hh in range(grp_heads):
                    hr = slice(RWKV_HEAD * hh, RWKV_HEAD * (hh + 1))
                    hl = slice(LANES * (hh // 2), LANES * (hh // 2 + 1))
                    blk = s0_ref[q, gw * g + RWKV_HEAD * hh:gw * g + RWKV_HEAD * (hh + 1), :]
                    piece = jnp.concatenate([blk, zero_blk] if hh % 2 == 0 else [zero_blk, blk], axis=1)
                    sbd[q, g, hr, hl] = piece
                    sbb[q, g, hr, hl] = piece.astype(BF16)
        carry[...] = sh_ref[...]

    x = x_ref[...].reshape(rows, D_MODEL)
    h = _rms_scale(x, g_ref[...]).astype(BF16)
    hb = _dot(perm_ref[...], h).astype(BF16)
    ones = ones_ref[...]

    is_first = (lax.broadcasted_iota(jnp.int32, (part, RWKV_SHIFT_W), 0) & (nt - 1)) == 0
    lane128 = lax.broadcasted_iota(jnp.int32, (part, 2 * LORA_RANK), 1)

    def prepare(r0):
        rs = slice(r0, r0 + part)
        q0 = r0 // nt
        pz = _dot(hb[rs], w_ref[:, w_off:])
        yield
        p = pz[:, :RWKV_SHIFT_W]
        z_s[rs, :] = pz[:, RWKV_SHIFT_W:]
        first = jnp.concatenate(
            [jnp.broadcast_to(carry[q0 + q:q0 + q + 1, :], (nt, RWKV_SHIFT_W)) for q in range(part // nt)], axis=0)
        prev = jnp.where(is_first, first, pltpu.roll(p, 1, 0))
        for q in range(part // nt):
            carry[q0 + q:q0 + q + 1, :] = p[nt * (q + 1) - 1:nt * (q + 1), :]
        pm = p + (prev - p) * mu_ref[...]
        yield
        r = pm[:, 0:BRANCH_W]
        k = pm[:, BRANCH_W:2 * BRANCH_W]
        v = pm[:, 2 * BRANCH_W:3 * BRANCH_W]
        ll = pm[:, 3 * BRANCH_W:]
        xl = jnp.where(lane128 < LORA_RANK, jnp.tanh(ll), ll).astype(BF16)
        dl = _dot(xl, wl_ref[...])
        yield
        w_log = -jax.nn.softplus(-(w0_ref[...] + dl[:, :BRANCH_W])) - 0.5
        ld = -jnp.exp(w_log)
        yield
        a = _sigmoid(a0_ref[...] + dl[:, BRANCH_W:])
        kk = k * kkw_ref[...]
        k2 = k * (1.0 + (a - 1.0) * kaw_ref[...])
        yield
        ssq, rk = _seg_sums([kk * kk, r * k2 * rkw_ref[...]], ones)
        hi, mid, lo = _split3(ld)
        cum3 = _dot(tri_ref[...], jnp.concatenate([hi, mid, lo], axis=1))
        yield
        kkn = kk * lax.rsqrt(jnp.maximum(ssq, 1e-24))
        cum = cum3[:, :BRANCH_W] + cum3[:, BRANCH_W:2 * BRANCH_W] + cum3[:, 2 * BRANCH_W:]
        e_incl = jnp.exp(cum)
        yield
        e_inv = jnp.exp(-cum)
        yield
        qk_s[rs, :] = (kkn * jnp.exp(cum - ld)).astype(BF16)
        qr_s[rs, :] = r * e_incl
        yield
        kb_s[rs, :] = (kkn * a * e_inv).astype(BF16)
        kk_s[rs, :] = (k2 * e_inv).astype(BF16)
        v_s[rs, :] = v.astype(BF16)
        ei_s[rs, :] = e_incl
        bonus_s[rs, :] = rk * v

    def chunk(c, g):
        r0 = c * CHUNK if isinstance(c, int) else pl.multiple_of(c * CHUNK, CHUNK)
        rs = pl.ds(r0, CHUNK)
        ls = slice(gw * g, gw * (g + 1))
        q_k = qk_s[rs, ls]
        q_r = qr_s[rs, ls]
        k_b = kb_s[rs, ls]
        k_k = kk_s[rs, ls]
        vb = v_s[rs, ls]
        q2 = jnp.concatenate([q_k, q_r.astype(BF16)], axis=0)
        bd_v = _block_rows(vb, head_masks)
        a_all = _dot_nt(q2, jnp.concatenate([_block_rows(k_b, head_masks), _block_rows(k_k, head_masks)], axis=0))
        yield
        a_kb = jnp.where(m_strict, a_all[:CHUNK, :gw], 0.0)
        a_rb = jnp.where(m_incl, a_all[CHUNK:, :gw], 0.0).astype(BF16)
        a_kk = jnp.where(m_strict, a_all[:CHUNK, gw:], 0.0).astype(BF16)
        a_rk = jnp.where(m_incl, a_all[CHUNK:, gw:], 0.0).astype(BF16)
        av = _dot(jnp.concatenate([a_kk, a_rk], axis=0), bd_v)
        npow = -a_kb
        tinv = eye_sbs + npow
        npow = _dot(npow.astype(BF16), _block_rows(npow, head_masks))
        yield
        akv = av[:CHUNK]
        o_v = av[CHUNK:]
        for it in range(1, log_seq):
            bd_n = _block_rows(npow, head_masks)
            if it + 1 < log_seq:
                both = _dot(jnp.concatenate([npow.astype(BF16), tinv.astype(BF16)], axis=0), bd_n)
                yield
                npow = both[:CHUNK]
                tinv = tinv + both[CHUNK:]
            else:
                last = _dot(tinv.astype(BF16), bd_n)
                yield
                tinv = tinv + last
        tw = _dot(tinv.astype(BF16),
                  jnp.concatenate([_block_rows(q_k, head_masks), _block_rows(-akv, head_masks)], axis=1))
        yield
        qt_k = tw[:, :gw]
        w_loc = tw[:, gw:]
        d1 = _dot(a_rb, jnp.concatenate([_block_rows(qt_k, head_masks), _block_rows(w_loc, head_masks)], axis=1))
        yield
        qh_r = q_r - d1[:, :gw]
        o_loc = d1[:, gw:] + o_v

        def update_state(sq, zz, pc):
            for hh in range(grp_heads):
                hr = slice(RWKV_HEAD * hh, RWKV_HEAD * (hh + 1))
                hl = slice(LANES * (hh // 2), LANES * (hh // 2 + 1))
                piece = (sbd[sq, g, hr, hl] + jnp.where(half_masks[hh % 2], zz[hr, hl], 0.0)) * pc[:, hl]
                sbd[sq, g, hr, hl] = piece
                sbb[sq, g, hr, hl] = piece.astype(BF16)

        if spc == 1:
            xs = _dot_nt(jnp.concatenate([qt_k.astype(BF16), qh_r.astype(BF16)], axis=0), sbb[c, g])
            yield
            u = w_loc - xs[:CHUNK]
            o_s[rs, ls] = xs[CHUNK:] + o_loc
            zz = _dot_tn(jnp.concatenate([u.astype(BF16), vb], axis=0), jnp.concatenate([k_b, k_k], axis=0))
            yield
            update_state(c, zz, ei_s[pl.ds(r0 + CHUNK - 1, 1), ls])
        else:
            kb_f = k_b.astype(F32)
            kk_f = k_k.astype(F32)
            v_f = vb.astype(F32)
            seqs = [slice(nt * q, nt * (q + 1)) for q in range(spc)]
            xs = [_dot_nt(jnp.concatenate([qt_k[qs], qh_r[qs]], axis=0).astype(BF16), sbb[c * spc + q, g])
                  for q, qs in enumerate(seqs)]
            yield
            o_s[rs, ls] = jnp.concatenate([x[nt:] for x in xs], axis=0) + o_loc
            zz = [_dot_tn(jnp.concatenate([w_loc[qs] - x[:nt], v_f[qs]], axis=0).astype(BF16),
                          jnp.concatenate([kb_f[qs], kk_f[qs]], axis=0).astype(BF16))
                  for x, qs in zip(xs, seqs)]
            yield
            for q in range(spc):
                update_state(c * spc + q, zz[q], ei_s[pl.ds(r0 + nt * (q + 1) - 1, 1), ls])

    def run_together(live):
        while live:
            still = []
            for gen in live:
                try:
                    next(gen)
                    still.append(gen)
                except StopIteration:
                    pass
            live = still

    run_together([prepare(r0) for r0 in range(0, rows, part)])
    if n_ch <= RW_UNROLL:
        run_together([chunk(c, g) for c in range(n_ch) for g in range(n_grp)])
    else:
        def loop_body(cc, carry_):
            run_together([chunk(cc * RW_UNROLL + i, g) for i in range(RW_UNROLL) for g in range(n_grp)])
            return carry_
        lax.fori_loop(0, n_ch // RW_UNROLL, loop_body, 0)

    ybs = []
    for r0 in range(0, rows, part):
        rs = slice(r0, r0 + part)
        o = o_s[rs, :]
        mean = _seg_sums([o], ones)[0] * (1.0 / RWKV_HEAD)
        dev = o - mean
        var = _seg_sums([dev * dev], ones)[0] * (1.0 / RWKV_HEAD)
        o = dev * lax.rsqrt(var + GN_EPS) * gnw_ref[...] + gnb_ref[...] + bonus_s[rs, :]
        ybs.append((o * _silu(z_s[rs, :])).astype(BF16))
    y_tm = _dot(permt_ref[...], jnp.concatenate(ybs, axis=0))
    y_ref[...] = y_tm.astype(y_ref.dtype).reshape(y_ref.shape)
    shout_ref[...] = carry[...]

    @pl.when(j == pl.num_programs(1) - 1)
    def _():
        for q in range(nbb):
            for g in range(n_grp):
                for hh in range(grp_heads):
                    hr = slice(RWKV_HEAD * hh, RWKV_HEAD * (hh + 1))
                    hl = slice(RWKV_HEAD * hh, RWKV_HEAD * (hh + 1))
                    sout_ref[q, gw * g + RWKV_HEAD * hh:gw * g + RWKV_HEAD * (hh + 1), :] = sbd[q, g, hr, hl]


def _rw_constants(nt, nbb, part):
    rows = nt * nbb
    r = np.arange(rows)
    src = (r % nt) * nbb + (r // nt)
    perm = np.zeros((rows, rows), np.float32)
    perm[r, src] = 1.0
    t = np.arange(part)
    tri = ((t[:, None] // nt) == (t[None, :] // nt)) & (t[None, :] <= t[:, None])
    return jnp.asarray(perm, BF16), jnp.asarray(perm.T, BF16), jnp.asarray(tri, BF16)


def _rw_call(x, shift0, s0, wl, *, l, sl, seq, batch, nt, nbb, y_dtype):
    rows = nt * nbb
    part = min(rows, 256)
    gw = RW_GROUP if nt == CHUNK else MXU_TILE
    grid = (batch // nbb, seq // nt)
    if nbb == batch:
        x_spec = pl.BlockSpec((rows, D_MODEL), lambda i, j: (j, 0))
        y_spec = pl.BlockSpec((rows, BRANCH_W), lambda i, j: (j, 0))
        y_shape = jax.ShapeDtypeStruct((seq * batch, BRANCH_W), y_dtype)
    else:
        x_spec = pl.BlockSpec((nt, nbb, D_MODEL), lambda i, j: (j, i, 0))
        y_spec = pl.BlockSpec((nt, nbb, BRANCH_W), lambda i, j: (j, i, 0))
        y_shape = jax.ShapeDtypeStruct((seq, batch, BRANCH_W), y_dtype)
    st_in = pl.BlockSpec((None, nbb, BRANCH_W, RWKV_HEAD), lambda i, j: (sl, i, 0, 0))
    sh_in = pl.BlockSpec((None, nbb, RWKV_SHIFT_W), lambda i, j: (sl, i, 0))
    st_spec = pl.BlockSpec((nbb, BRANCH_W, RWKV_HEAD), lambda i, j: (i, 0, 0))
    sh_spec = pl.BlockSpec((nbb, RWKV_SHIFT_W), lambda i, j: (i, 0))
    perm, permt, tri = _rw_constants(nt, nbb, part)
    lw = lambda name: _weight_operand(wl[name], l)
    const = lambda a: (a, _full_spec(a.shape))
    operands = (lw("g_pre"), lw("w_rw"), const(perm), const(permt), lw("rw_mu"), lw("rw_w0"), lw("rw_lora"),
                lw("rw_a0"), lw("rw_kk"), lw("rw_ka"), lw("rw_rk"), lw("rw_gnw"), lw("rw_gnb"),
                const(wl["rw_ones"]), const(tri))
    weights = [a for a, _ in operands]
    work = lambda dt: pltpu.VMEM((rows, BRANCH_W), dt)
    return pl.pallas_call(
        functools.partial(_rw_body, nt=nt, nbb=nbb, part=part, gw=gw, w_off=wl["w_rw"].offset),
        grid=grid,
        in_specs=[x_spec, st_in, sh_in] + [spec for _, spec in operands],
        out_specs=[y_spec, st_spec, sh_spec],
        out_shape=[y_shape, jax.ShapeDtypeStruct((batch, BRANCH_W, RWKV_HEAD), F32),
                   jax.ShapeDtypeStruct((batch, RWKV_SHIFT_W), F32)],
        scratch_shapes=[pltpu.VMEM((nbb, BRANCH_W // gw, gw, gw), F32),
                        pltpu.VMEM((nbb, BRANCH_W // gw, gw, gw), BF16),
                        pltpu.VMEM((nbb, RWKV_SHIFT_W), F32),
                        work(BF16), work(F32), work(BF16), work(BF16), work(BF16), work(F32), work(F32),
                        work(F32), work(F32)],
        compiler_params=_params(("arbitrary", "arbitrary")),
    )(x, s0, shift0, *weights)


def _merge_body(x_ref, ys5_ref, yrw_ref, ypl_ref, gpre_ref, wg_ref, wu_ref, wo_ref, gpost_ref, o_ref):
    x = x_ref[...]
    h = _rms_scale(x, gpre_ref[...]).astype(BF16)
    merged = None
    for i, y_ref in enumerate((ys5_ref, yrw_ref, ypl_ref)):
        gate = _sigmoid(_dot(h, wg_ref[:, D_MODEL * i:D_MODEL * (i + 1)]))
        term = gate * _dot(y_ref[...].astype(BF16), wu_ref[i])
        merged = term if merged is None else merged + term
    out = _dot(merged.astype(BF16), wo_ref[...])
    o_ref[...] = x + _rms_scale(out, gpost_ref[...])


def _merge_call(x, ys5, yrw, ypl, wl, *, l, tm):
    m = x.shape[0]
    tm = min(tm, m)
    weights, w_specs = zip(*[_weight_operand(wl[k], l) for k in ("g_pre", "w_gate", "w_up", "w_o", "g_post")])
    row_spec = lambda w: pl.BlockSpec((tm, w), lambda i: (i, 0))
    return pl.pallas_call(
        _merge_body,
        grid=(m // tm,),
        in_specs=[row_spec(D_MODEL), row_spec(BRANCH_W), row_spec(BRANCH_W), row_spec(BRANCH_W)]
        + list(w_specs),
        out_specs=row_spec(D_MODEL),
        out_shape=jax.ShapeDtypeStruct((m, D_MODEL), F32),
        compiler_params=_params(("parallel",)),
    )(x, ys5, yrw, ypl, *weights)


def _block_diag(blocks):
    *lead, n, r, c = blocks.shape
    eye = jnp.eye(n, dtype=blocks.dtype)
    return jnp.einsum("...grc,gh->...grhc", blocks, eye).reshape(*lead, n * r, n * c)


def _prep_weights(norm_pre, norm_post, w_in, s5_lam_re, s5_lam_im, s5_log_dt, s5_b_re, s5_b_im,
                  s5_c_re, s5_c_im, s5_d, s5_w_glu, rwkv_mu, rwkv_w0, rwkv_w_w2, rwkv_a0, rwkv_w_a2,
                  rwkv_k_k, rwkv_k_a, rwkv_r_k, rwkv_gn_w, rwkv_gn_b, pool_w, pool_scale,
                  w_up_s5, w_up_rwkv, w_up_pool, w_o):
    depth = w_in.shape[0]
    row = lambda a: a.astype(F32).reshape(depth, 1, -1)
    wl = {}
    wl["g_pre"] = row(norm_pre)
    wl["g_post"] = row(norm_post)
    c_rw = 2 * BRANCH_W
    c_pool = c_rw + RWKV_SHIFT_W + BRANCH_W
    c_gate = c_pool + 2 * BRANCH_W
    w16 = w_in.astype(BF16)

    def window(start, end):
        n = max(n for n in range(1, end // LANES + 1)
                if end % n == 0 and (end // n) % LANES == 0 and (n - 1) * (end // n) <= start)
        return _ColWindow(w16, end // n, n - 1, start - (n - 1) * (end // n))

    wl["w_s5"] = window(0, c_rw)
    wl["w_rw"] = window(c_rw, c_pool)
    wl["w_pool"] = window(c_pool, c_gate)
    wl["w_gate"] = w16[:, :, c_gate:]

    lr = s5_lam_re.astype(F32)
    li = s5_lam_im.astype(F32)
    dt = jnp.exp(s5_log_dt.astype(F32))[:, :, None]
    mag = jnp.exp(lr * dt)
    ab_r = mag * jnp.cos(li * dt)
    ab_i = mag * jnp.sin(li * dt)
    den = lr * lr + li * li
    nr = ab_r - 1.0
    co_r = ((nr * lr + ab_i * li) / den)[..., None]
    co_i = ((ab_i * lr - nr * li) / den)[..., None]
    b_re = s5_b_re.astype(F32)
    b_im = s5_b_im.astype(F32)
    bf_re = co_r * b_re - co_i * b_im
    bf_im = co_r * b_im + co_i * b_re
    gh = S5_GROUPS // 2
    wb = []
    for k in range(2):
        gs = slice(gh * k, gh * (k + 1))
        wb.append(jnp.concatenate([_block_diag(jnp.swapaxes(bf_re[:, gs], 2, 3)),
                                   _block_diag(jnp.swapaxes(bf_im[:, gs], 2, 3))], axis=2))
    wl["s5_wb"] = jnp.stack(wb, axis=1).astype(BF16)
    wl["s5_ar"] = ab_r.reshape(depth, 1, S5_NSTATE)
    wl["s5_ai"] = ab_i.reshape(depth, 1, S5_NSTATE)
    c_re = s5_c_re.astype(F32)
    c_im = s5_c_im.astype(F32)
    cc = []
    for k in range(2):
        gs = slice(gh * k, gh * (k + 1))
        cc.append(jnp.concatenate([_block_diag(jnp.swapaxes(c_re[:, gs], 2, 3)),
                                   _block_diag(jnp.swapaxes(-c_im[:, gs], 2, 3))], axis=1))
    wl["s5_cc"] = jnp.stack(cc, axis=1).astype(BF16)
    wl["s5_d"] = row(s5_d)
    wl["s5_wglu"] = s5_w_glu.astype(BF16)

    wl["rw_mu"] = row(rwkv_mu)
    wl["rw_w0"] = row(rwkv_w0)
    zeros = jnp.zeros((depth, LORA_RANK, BRANCH_W), F32)
    wl["rw_lora"] = jnp.concatenate(
        [jnp.concatenate([rwkv_w_w2.astype(F32), zeros], axis=2),
         jnp.concatenate([zeros, rwkv_w_a2.astype(F32)], axis=2)], axis=1).astype(BF16)
    wl["rw_a0"] = row(rwkv_a0)
    wl["rw_kk"] = row(rwkv_k_k)
    wl["rw_ka"] = row(rwkv_k_a)
    wl["rw_rk"] = row(rwkv_r_k)
    wl["rw_gnw"] = row(rwkv_gn_w)
    wl["rw_gnb"] = row(rwkv_gn_b)
    wl["rw_ones"] = _block_diag(jnp.ones((MXU_TILE // RWKV_HEAD, RWKV_HEAD, RWKV_HEAD), F32)).astype(BF16)

    wl["pool_w"] = _block_diag(pool_w.astype(F32)).astype(BF16)
    wl["pool_scale"] = row(pool_scale)
    wl["w_up"] = jnp.stack([w_up_s5, w_up_rwkv, w_up_pool], axis=1).astype(BF16)
    wl["w_o"] = w_o.astype(BF16)
    return wl


def _layer(x, states, wl, *, l, sl, seq, batch, start, prompt):
    s5r, s5i, rws, rwsh, pbuf = states
    if prompt:
        nt, nb, nbb, y_dtype = CHUNK, batch, batch, BF16
        nts = min(seq, 2 * CHUNK)
    else:
        nt, nb, nbb, y_dtype = seq, min(batch, 32), min(batch, 16), F32
        nts = seq
    x3 = x.reshape(seq, batch, D_MODEL)
    flat = lambda y: y.reshape(seq * batch, BRANCH_W)
    if nb == batch:
        x_tm = x
        pbuf_in = pbuf.reshape(pbuf.shape[0], POOL_BUF * batch, BRANCH_W)
    else:
        x_tm = x3
        pbuf_in = pbuf
    ys5, n_re, n_im = _s5_call(x_tm, wl, s5r, s5i, l=l, sl=sl, seq=seq, batch=batch, nt=nts, nb=nb, y_dtype=y_dtype)
    ypl, n_buf = _pool_call(x_tm, wl, pbuf_in, l=l, sl=sl, seq=seq, batch=batch, nt=nts, nb=nb, start=start,
                            y_dtype=y_dtype)
    yrw, n_s, n_shift = _rw_call(x if nbb == batch else x3, rwsh, rws, wl, l=l, sl=sl, seq=seq, batch=batch,
                                 nt=nt, nbb=nbb, y_dtype=y_dtype)
    x_new = _merge_call(x, flat(ys5), flat(yrw), flat(ypl), wl, l=l, tm=1024)
    n_buf = n_buf.reshape(POOL_BUF, batch, BRANCH_W)
    return x_new, (n_re, n_im, n_s, n_shift, n_buf)


def kernel(x_prompt, x_sample, state_s5_re, state_s5_im, state_rwkv, state_shift, state_pool, norm_pre, norm_post, w_in, s5_lam_re, s5_lam_im, s5_log_dt, s5_b_re, s5_b_im, s5_c_re, s5_c_im, s5_d, s5_w_glu, rwkv_mu, rwkv_w0, rwkv_w_w2, rwkv_a0, rwkv_w_a2, rwkv_k_k, rwkv_k_a, rwkv_r_k, rwkv_gn_w, rwkv_gn_b, pool_w, pool_scale, w_up_s5, w_up_rwkv, w_up_pool, w_o):
    depth = w_in.shape[0]
    bp, tp, _ = x_prompt.shape
    bs, ts, _ = x_sample.shape
    xp = jnp.transpose(x_prompt, (1, 0, 2)).reshape(tp * bp, D_MODEL)
    xs = jnp.transpose(x_sample, (1, 0, 2)).reshape(ts * bs, D_MODEL)
    outs_p = [[] for _ in range(5)]
    outs_s = [[] for _ in range(5)]
    wl = _prep_weights(norm_pre, norm_post, w_in, s5_lam_re, s5_lam_im, s5_log_dt, s5_b_re, s5_b_im,
                       s5_c_re, s5_c_im, s5_d, s5_w_glu, rwkv_mu, rwkv_w0, rwkv_w_w2, rwkv_a0, rwkv_w_a2,
                       rwkv_k_k, rwkv_k_a, rwkv_r_k, rwkv_gn_w, rwkv_gn_b, pool_w, pool_scale,
                       w_up_s5, w_up_rwkv, w_up_pool, w_o)
    zeros_p = (jnp.zeros((1, bp, S5_NSTATE), F32), jnp.zeros((1, bp, S5_NSTATE), F32),
               jnp.zeros((1, bp, BRANCH_W, RWKV_HEAD), F32), jnp.zeros((1, bp, RWKV_SHIFT_W), F32),
               jnp.zeros((1, POOL_BUF, bp, BRANCH_W), F32))
    for l in range(depth):
        xp, new_p = _layer(xp, zeros_p, wl, l=l, sl=0, seq=tp, batch=bp, start=0, prompt=True)
        st_s = (state_s5_re[l].reshape(1, bs, S5_NSTATE), state_s5_im[l].reshape(1, bs, S5_NSTATE),
                state_rwkv[l].reshape(1, bs, BRANCH_W, RWKV_HEAD), state_shift[l][None],
                jnp.transpose(state_pool[l], (1, 0, 2))[None])
        xs, new_s = _layer(xs, st_s, wl, l=l, sl=0, seq=ts, batch=bs, start=PAST_LEN, prompt=False)
        for lst, val in zip(outs_p, new_p):
            lst.append(val)
        for lst, val in zip(outs_s, new_s):
            lst.append(val)

    def finish(outs, b):
        re, im, rw, sh, pb = [jnp.stack(v) for v in outs]
        return (re.reshape(depth, b, S5_GROUPS, S5_STATE), im.reshape(depth, b, S5_GROUPS, S5_STATE),
                rw.reshape(depth, b, RWKV_HEADS, RWKV_HEAD, RWKV_HEAD), sh,
                jnp.transpose(pb, (0, 2, 1, 3)))

    y_p = jnp.transpose(xp.reshape(tp, bp, D_MODEL), (1, 0, 2))
    y_s = jnp.transpose(xs.reshape(ts, bs, D_MODEL), (1, 0, 2))
    return (y_p, y_s) + finish(outs_p, bp) + finish(outs_s, bs)
```
